```python
import jax, jax.numpy as jnp
from jax import lax
import numpy as np

D_MODEL = 1024
BATCH = 2
SEQ = 8192
DEPTH = 4
DEC_BATCH = 128
DEC_SEQ = 1
PAST_LEN = 2048
PAGE_SIZE = 128

N_AB_LAYERS = (DEPTH + 1) // 2
N_C_LAYERS = DEPTH // 2

A_HEADS = 8
A_HEAD_DIM = D_MODEL // A_HEADS
A_WIDTH = A_HEADS * A_HEAD_DIM
A_PATTERNS = ((128, 1), (512, 4), (2048, 16))
A_MAX_WINDOW = 2048
A_Q_BLOCK = 128
ROPE_THETA = 10000.0

B_HEADS = 4
B_WIDTH = D_MODEL
B_HEAD_DIM = B_WIDTH // B_HEADS
B_QKV_BLOCK = 4
B_CHUNK = 128

C_WIDTH = 3 * D_MODEL // 2
C_BLOCKS = 12
C_BLOCK = C_WIDTH // C_BLOCKS
RG_LRU_C = 8.0

CONV_WIDTH = 4
AB_IN = 4 * A_WIDTH + 2 * B_WIDTH
C_IN = 2 * C_WIDTH
NORM_EPS = 1e-6

kernel_name = 'hybrid_dilated_attn_mlstm_rglru_step'


def rmsnorm(x, g):
    xf = x.astype(jnp.float32)
    y = xf * lax.rsqrt(jnp.mean(xf * xf, axis=-1, keepdims=True) + NORM_EPS)
    return (y * g.astype(jnp.float32)).astype(x.dtype)


def head_layernorm(x):
    mu = jnp.mean(x, axis=-1, keepdims=True)
    xc = x - mu
    return xc * lax.rsqrt(jnp.mean(xc * xc, axis=-1, keepdims=True) + 1e-5)


def rope(x, pos):
    half = x.shape[-1] // 2
    freqs = ROPE_THETA ** (-jnp.arange(half, dtype=jnp.float32) / half)
    ang = pos.astype(jnp.float32)[:, None] * freqs[None, :]
    cos = jnp.cos(ang)[None, :, None, :]
    sin = jnp.sin(ang)[None, :, None, :]
    xf = x.astype(jnp.float32)
    x1, x2 = xf[..., :half], xf[..., half:]
    return jnp.concatenate([x1 * cos - x2 * sin, x2 * cos + x1 * sin], axis=-1).astype(x.dtype)


def causal_dwconv(x, buf, w, b):
    xp = jnp.concatenate([buf.astype(x.dtype), x], axis=1)
    y = lax.conv_general_dilated(xp, w.astype(x.dtype)[:, None, :], window_strides=(1,), padding='VALID',
                                 dimension_numbers=('NWC', 'WIO', 'NWC'), feature_group_count=x.shape[-1])
    return y + b.astype(x.dtype), xp[:, xp.shape[1] - (CONV_WIDTH - 1):]


def blockdiag(x, w):
    xs = x.reshape(x.shape[:-1] + (w.shape[0], w.shape[1]))
    return jnp.einsum('...ni,nio->...no', xs, w).reshape(x.shape[:-1] + (w.shape[0] * w.shape[2],))


def dilated_attend(q, kf, vf, base, min_valid):
    qi = jnp.arange(q.shape[1])
    scale = A_HEAD_DIM ** -0.5
    outs, lses = [], []
    for window, dil in A_PATTERNS:
        n_keys = window // dil + 1
        idx = base + qi[:, None] - dil * jnp.arange(n_keys)[None, :]
        valid = idx >= min_valid
        idx = jnp.maximum(idx, 0)
        kg = jnp.take(kf, idx, axis=1)
        vg = jnp.take(vf, idx, axis=1)
        s = jnp.einsum('bqhd,bqjhd->bhqj', q, kg, preferred_element_type=jnp.float32) * scale
        s = jnp.where(valid[None, None], s, -jnp.inf)
        lse = jax.nn.logsumexp(s, axis=-1)
        p = jnp.exp(s - lse[..., None])
        outs.append(jnp.einsum('bhqj,bqjhd->bqhd', p, vg.astype(jnp.float32)))
        lses.append(lse)
    wts = jax.nn.softmax(jnp.stack(lses, axis=0), axis=0)
    wts = jnp.transpose(wts, (0, 1, 3, 2))[..., None]
    return jnp.sum(jnp.stack(outs, axis=0) * wts, axis=0)


def dilated_attention_prompt(q, k, v):
    bsz, L = q.shape[0], q.shape[1]
    pad = jnp.zeros((bsz, A_MAX_WINDOW) + k.shape[2:], k.dtype)
    kf = jnp.concatenate([pad, k], axis=1)
    vf = jnp.concatenate([pad, v], axis=1)
    nblk = L // A_Q_BLOCK
    qb = jnp.swapaxes(q.reshape((bsz, nblk, A_Q_BLOCK) + q.shape[2:]), 0, 1)

    def one_block(args):
        qblk, bi = args
        return dilated_attend(qblk, kf, vf, A_MAX_WINDOW + bi * A_Q_BLOCK, A_MAX_WINDOW)

    o = lax.map(one_block, (qb, jnp.arange(nblk, dtype=jnp.int32)))
    return jnp.swapaxes(o, 0, 1).reshape((bsz, L) + q.shape[2:])


def mlstm_chunk(q, k, v, logi, logf, c0, n0, m0):
    T = q.shape[1]
    b = jnp.cumsum(logf, axis=1)
    dmat = b[:, :, None, :] - b[:, None, :, :] + logi[:, None, :, :]
    causal = jnp.tril(jnp.ones((T, T), dtype=bool))
    dmat = jnp.where(causal[None, :, :, None], dmat, -jnp.inf)
    inter = b + m0[:, None, :]
    m = jnp.maximum(jnp.max(dmat, axis=2), inter)
    w_intra = jnp.exp(dmat - m[:, :, None, :])
    w_inter = jnp.exp(inter - m)
    s = jnp.einsum('bthd,bjhd->btjh', q, k) * w_intra
    num = jnp.einsum('btjh,bjhe->bthe', s, v) + w_inter[..., None] * jnp.einsum('bthd,bhde->bthe', q, c0)
    den = jnp.sum(s, axis=2) + w_inter * jnp.einsum('bthd,bhd->bth', q, n0)
    h = num / jnp.maximum(jnp.abs(den), jnp.exp(-m))[..., None]
    bl = b[:, -1]
    dl = bl[:, None, :] - b + logi
    m_new = jnp.maximum(bl + m0, jnp.max(dl, axis=1))
    wj = jnp.exp(dl - m_new[:, None, :])
    carry = jnp.exp(bl + m0 - m_new)
    c_new = carry[..., None, None] * c0 + jnp.einsum('bjh,bjhd,bjhe->bhde', wj, k, v)
    n_new = carry[..., None] * n0 + jnp.einsum('bjh,bjhd->bhd', wj, k)
    return h, c_new, n_new, m_new


def mlstm_scan(q, k, v, logi, logf, c0, n0, m0):
    bsz, L = q.shape[0], q.shape[1]
    chunk = B_CHUNK if L % B_CHUNK == 0 else L
    nc = L // chunk

    def to_chunks(a):
        return jnp.swapaxes(a.reshape((bsz, nc, chunk) + a.shape[2:]), 0, 1)

    def step(carry, xs):
        h, c, n, m = mlstm_chunk(xs[0], xs[1], xs[2], xs[3], xs[4], carry[0], carry[1], carry[2])
        return (c, n, m), h

    (c, n, m), h = lax.scan(step, (c0, n0, m0),
                            (to_chunks(q), to_chunks(k), to_chunks(v), to_chunks(logi), to_chunks(logf)))
    h = jnp.swapaxes(h, 0, 1).reshape((bsz, L) + h.shape[3:])
    return h, c, n, m


def _linrec_combine(e1, e2):
    a1, b1 = e1
    a2, b2 = e2
    return a1 * a2, a2 * b1 + b2


def rglru(x, h0, w_a, b_a, w_x, b_x, lam):
    f32 = jnp.float32
    xf = x.astype(f32)
    r = jax.nn.sigmoid(blockdiag(xf, w_a.astype(f32)) + b_a.astype(f32))
    i = jax.nn.sigmoid(blockdiag(xf, w_x.astype(f32)) + b_x.astype(f32))
    log_a = -RG_LRU_C * r * jax.nn.softplus(-lam.astype(f32))
    a = jnp.exp(log_a)
    u = jnp.sqrt(-jnp.expm1(2.0 * log_a)) * (i * xf)
    u = u.at[:, 0].add(a[:, 0] * h0.astype(f32))
    _, h = lax.associative_scan(_linrec_combine, (a, u), axis=1)
    return h, h[:, -1]


def ab_mixer(h, pos, kv_past, c0, n0, m0, conv_buf, w_in, w_out, conv_w, conv_b, wq, wk, wv, w_g, b_g,
             out_norm, skip):
    f32 = jnp.float32
    bsz, L, _ = h.shape
    proj = jnp.einsum('bld,de->ble', h, w_in)
    qa, ka, va, za, xb, zb = jnp.split(
        proj, [A_WIDTH, 2 * A_WIDTH, 3 * A_WIDTH, 4 * A_WIDTH, 4 * A_WIDTH + B_WIDTH], axis=-1)
    ashape = (bsz, L, A_HEADS, A_HEAD_DIM)
    qa = rope(qa.reshape(ashape), pos)
    ka = rope(ka.reshape(ashape), pos)
    va = va.reshape(ashape)
    if kv_past is None:
        oa = dilated_attention_prompt(qa, ka, va)
        keep = min(A_MAX_WINDOW, L)
        new_k, new_v = ka[:, L - keep:], va[:, L - keep:]
    else:
        pk, pv = kv_past
        kf = jnp.concatenate([pk.astype(ka.dtype), ka], axis=1)
        vf = jnp.concatenate([pv.astype(va.dtype), va], axis=1)
        oa = dilated_attend(qa, kf, vf, pk.shape[1], 0)
        new_k, new_v = ka, va
    ya = oa.reshape(bsz, L, A_WIDTH).astype(h.dtype) * jax.nn.silu(za)
    xconv, new_buf = causal_dwconv(xb, conv_buf, conv_w, conv_b)
    xc = jax.nn.silu(xconv)
    qb = blockdiag(xc, wq)
    kb = blockdiag(xc, wk)
    vb = blockdiag(xb, wv)
    g = (jnp.einsum('ble,eg->blg', jnp.concatenate([qb, kb, vb], axis=-1), w_g) + b_g).astype(f32)
    logi = g[..., :B_HEADS]
    logf = jax.nn.log_sigmoid(g[..., B_HEADS:])
    bshape = (bsz, L, B_HEADS, B_HEAD_DIM)
    hb, c1, n1, m1 = mlstm_scan(qb.reshape(bshape).astype(f32),
                                kb.reshape(bshape).astype(f32) * (B_HEAD_DIM ** -0.5),
                                vb.reshape(bshape).astype(f32), logi, logf,
                                c0.astype(f32), n0.astype(f32), m0.astype(f32))
    hb = head_layernorm(hb).reshape(bsz, L, B_WIDTH) * out_norm.astype(f32) + skip.astype(f32) * xc.astype(f32)
    yb = hb.astype(h.dtype) * jax.nn.silu(zb)
    y = jnp.einsum('ble,ed->bld', jnp.concatenate([ya, yb], axis=-1), w_out)
    return y, (new_k, new_v, c1, n1, m1, new_buf)


def c_mixer(h, h0, conv_buf, w_in, w_out, conv_w, conv_b, w_a, b_a, w_x, b_x, lam):
    proj = jnp.einsum('bld,de->ble', h, w_in)
    xc, zc = jnp.split(proj, [C_WIDTH], axis=-1)
    xconv, new_buf = causal_dwconv(xc, conv_buf, conv_w, conv_b)
    hr, h_last = rglru(xconv, h0, w_a, b_a, w_x, b_x, lam)
    y = jnp.einsum('ble,ed->bld', hr.astype(h.dtype) * jax.nn.silu(zc), w_out)
    return y, (h_last, new_buf)


def trunk(x, pos, past_k, past_v, st, abp, cp):
    outs = [[] for _ in range(8)]
    for layer in range(DEPTH):
        li = layer // 2
        if layer % 2 == 0:
            h = rmsnorm(x, abp[0][li])
            kv = None if past_k is None else (past_k[li], past_v[li])
            y, new = ab_mixer(h, pos, kv, st[0][li], st[1][li], st[2][li], st[3][li],
                              *[w[li] for w in abp[2:]])
            x = x + rmsnorm(y, abp[1][li])
            for j in range(6):
                outs[j].append(new[j])
        else:
            h = rmsnorm(x, cp[0][li])
            y, new = c_mixer(h, st[4][li], st[5][li], *[w[li] for w in cp[2:]])
            x = x + rmsnorm(y, cp[1][li])
            outs[6].append(new[0])
            outs[7].append(new[1])
    return x, [jnp.stack(o, axis=0) for o in outs]


def setup_inputs(seed: int = 0) -> dict:
    key = jax.random.key(seed)
    ks = iter(jax.random.split(key, 40))
    f32 = jnp.float32

    def nrm(shape, scale):
        return jax.random.normal(next(ks), shape, f32) * scale

    win_buf = min(A_MAX_WINDOW, PAST_LEN)
    na, nc = N_AB_LAYERS, N_C_LAYERS
    d = {}
    d['x_prompt'] = nrm((BATCH, SEQ, D_MODEL), 1.0)
    d['x_sample'] = nrm((DEC_BATCH, DEC_SEQ, D_MODEL), 1.0)
    d['cache_k'] = nrm((na, DEC_BATCH, win_buf, A_HEADS, A_HEAD_DIM), 1.0)
    d['cache_v'] = nrm((na, DEC_BATCH, win_buf, A_HEADS, A_HEAD_DIM), 1.0)
    d['state_mlstm_c'] = nrm((na, DEC_BATCH, B_HEADS, B_HEAD_DIM, B_HEAD_DIM), 0.5)
    d['state_mlstm_n'] = nrm((na, DEC_BATCH, B_HEADS, B_HEAD_DIM), 0.5)
    d['state_mlstm_m'] = nrm((na, DEC_BATCH, B_HEADS), 0.5)
    d['state_mlstm_conv'] = nrm((na, DEC_BATCH, CONV_WIDTH - 1, B_WIDTH), 1.0)
    d['state_rglru_h'] = nrm((nc, DEC_BATCH, C_WIDTH), 0.5)
    d['state_rglru_conv'] = nrm((nc, DEC_BATCH, CONV_WIDTH - 1, C_WIDTH), 1.0)
    d['ab_norm_pre'] = 1.0 + nrm((na, D_MODEL), 0.05)
    d['ab_norm_post'] = 1.0 + nrm((na, D_MODEL), 0.05)
    d['ab_w_in'] = nrm((na, D_MODEL, AB_IN), D_MODEL ** -0.5)
    d['ab_w_out'] = nrm((na, A_WIDTH + B_WIDTH, D_MODEL), (A_WIDTH + B_WIDTH) ** -0.5)
    d['mlstm_conv_w'] = nrm((na, CONV_WIDTH, B_WIDTH), CONV_WIDTH ** -0.5)
    d['mlstm_conv_b'] = nrm((na, B_WIDTH), 0.02)
    nb = B_WIDTH // B_QKV_BLOCK
    d['mlstm_wq'] = nrm((na, nb, B_QKV_BLOCK, B_QKV_BLOCK), B_QKV_BLOCK ** -0.5)
    d['mlstm_wk'] = nrm((na, nb, B_QKV_BLOCK, B_QKV_BLOCK), B_QKV_BLOCK ** -0.5)
    d['mlstm_wv'] = nrm((na, nb, B_QKV_BLOCK, B_QKV_BLOCK), B_QKV_BLOCK ** -0.5)
    d['mlstm_w_gates'] = nrm((na, 3 * B_WIDTH, 2 * B_HEADS), 0.3 * (3 * B_WIDTH) ** -0.5)
    f_bias = jnp.broadcast_to(jnp.linspace(3.0, 6.0, B_HEADS, dtype=f32), (na, B_HEADS))
    d['mlstm_b_gates'] = jnp.concatenate([nrm((na, B_HEADS), 0.1), f_bias + nrm((na, B_HEADS), 0.1)], axis=-1)
    d['mlstm_out_norm'] = 1.0 + nrm((na, B_WIDTH), 0.05)
    d['mlstm_skip'] = 1.0 + nrm((na, B_WIDTH), 0.05)
    d['c_norm_pre'] = 1.0 + nrm((nc, D_MODEL), 0.05)
    d['c_norm_post'] = 1.0 + nrm((nc, D_MODEL), 0.05)
    d['c_w_in'] = nrm((nc, D_MODEL, C_IN), D_MODEL ** -0.5)
    d['c_w_out'] = nrm((nc, C_WIDTH, D_MODEL), C_WIDTH ** -0.5)
    d['c_conv_w'] = nrm((nc, CONV_WIDTH, C_WIDTH), CONV_WIDTH ** -0.5)
    d['c_conv_b'] = nrm((nc, C_WIDTH), 0.02)
    d['c_w_a'] = nrm((nc, C_BLOCKS, C_BLOCK, C_BLOCK), C_BLOCK ** -0.5)
    d['c_b_a'] = nrm((nc, C_WIDTH), 0.02)
    d['c_w_x'] = nrm((nc, C_BLOCKS, C_BLOCK, C_BLOCK), C_BLOCK ** -0.5)
    d['c_b_x'] = nrm((nc, C_WIDTH), 0.02)
    a0 = jax.random.uniform(next(ks), (nc, C_WIDTH), f32, minval=0.9, maxval=0.999)
    p = a0 ** (1.0 / RG_LRU_C)
    d['c_lambda'] = jnp.log(p) - jnp.log1p(-p)
    return d


def reference(x_prompt, x_sample, cache_k, cache_v, state_mlstm_c, state_mlstm_n, state_mlstm_m,
              state_mlstm_conv, state_rglru_h, state_rglru_conv, ab_norm_pre, ab_norm_post, ab_w_in, ab_w_out,
              mlstm_conv_w, mlstm_conv_b, mlstm_wq, mlstm_wk, mlstm_wv, mlstm_w_gates, mlstm_b_gates,
              mlstm_out_norm, mlstm_skip, c_norm_pre, c_norm_post, c_w_in, c_w_out, c_conv_w, c_conv_b,
              c_w_a, c_b_a, c_w_x, c_b_x, c_lambda):
    f32 = jnp.float32
    abp = (ab_norm_pre, ab_norm_post, ab_w_in, ab_w_out, mlstm_conv_w, mlstm_conv_b, mlstm_wq, mlstm_wk,
           mlstm_wv, mlstm_w_gates, mlstm_b_gates, mlstm_out_norm, mlstm_skip)
    cp = (c_norm_pre, c_norm_post, c_w_in, c_w_out, c_conv_w, c_conv_b, c_w_a, c_b_a, c_w_x, c_b_x, c_lambda)
    bp = x_prompt.shape[0]
    zero_st = (jnp.zeros((N_AB_LAYERS, bp, B_HEADS, B_HEAD_DIM, B_HEAD_DIM), f32),
               jnp.zeros((N_AB_LAYERS, bp, B_HEADS, B_HEAD_DIM), f32),
               jnp.zeros((N_AB_LAYERS, bp, B_HEADS), f32),
               jnp.zeros((N_AB_LAYERS, bp, CONV_WIDTH - 1, B_WIDTH), x_prompt.dtype),
               jnp.zeros((N_C_LAYERS, bp, C_WIDTH), f32),
               jnp.zeros((N_C_LAYERS, bp, CONV_WIDTH - 1, C_WIDTH), x_prompt.dtype))
    pos_p = jnp.arange(x_prompt.shape[1], dtype=jnp.int32)
    pos_s = PAST_LEN + jnp.arange(x_sample.shape[1], dtype=jnp.int32)
    y_prompt, pst = trunk(x_prompt, pos_p, None, None, zero_st, abp, cp)
    p_k, p_v, p_mc, p_mn, p_mm, p_mconv, p_rh, p_rconv = pst
    s_in = (state_mlstm_c, state_mlstm_n, state_mlstm_m, state_mlstm_conv, state_rglru_h, state_rglru_conv)
    y_sample, sst = trunk(x_sample, pos_s, cache_k, cache_v, s_in, abp, cp)
    s_k, s_v, s_mc, s_mn, s_mm, s_mconv, s_rh, s_rconv = sst
    return (y_prompt, y_sample, p_k, p_v, p_mc, p_mn, p_mm, p_mconv, p_rh, p_rconv,
            s_k, s_v, s_mc, s_mn, s_mm, s_mconv, s_rh, s_rconv)
```

```python
import functools

import jax
import jax.numpy as jnp
import numpy as np
from jax import lax
from jax.experimental import pallas as pl
from jax.experimental.pallas import tpu as pltpu

F32 = jnp.float32
BF16 = jnp.bfloat16

D_MODEL = 1024
PAST_LEN = 2048
A_HEADS = 8
A_HEAD_DIM = 128
A_WIDTH = 1024
A_PATTERNS = ((128, 1), (512, 4), (2048, 16))
A_MAX_WINDOW = 2048
ROPE_THETA = 10000.0
B_HEADS = 4
B_WIDTH = 1024
B_HEAD_DIM = 256
B_QKV_BLOCK = 4
B_CHUNK = 128
C_WIDTH = 1536
C_BLOCKS = 12
C_BLOCK = 128
RG_LRU_C = 8.0
CONV_WIDTH = 4
AB_IN = 6 * 1024
C_IN = 2 * C_WIDTH
NORM_EPS = 1e-6

LANES = 128
SUBLANES = 8
VMEM_LIMIT = 56 * 1024 * 1024


def _cparams(*sem):
    return pltpu.CompilerParams(dimension_semantics=sem, vmem_limit_bytes=VMEM_LIMIT)


def _sigmoid(x):
    return 1.0 / (1.0 + jnp.exp(-x))


def _silu(x):
    return x * _sigmoid(x)


def _log_sigmoid(x):
    return jnp.minimum(x, 0.0) - jnp.log1p(jnp.exp(-jnp.abs(x)))


def _softplus(x):
    return jnp.maximum(x, 0.0) + jnp.log1p(jnp.exp(-jnp.abs(x)))


def _rmsnorm(x, g):
    return x * lax.rsqrt(jnp.mean(x * x, axis=-1, keepdims=True) + NORM_EPS) * g


def _dot(a, b):
    return jnp.dot(a, b, preferred_element_type=F32)


def _dot_nt(a, b):
    return lax.dot_general(a, b, (((1,), (1,)), ((), ())), preferred_element_type=F32)


def _dot_tn(a, b):
    return lax.dot_general(a, b, (((0,), (0,)), ((), ())), preferred_element_type=F32)


def _inproj_kernel(x_ref, g_ref, w_ref, cos_ref, sin_ref, o_ref, h_scr, *, n_rope):
    j = pl.program_id(1)

    @pl.when(j == 0)
    def _():
        h_scr[...] = _rmsnorm(x_ref[...], g_ref[...]).astype(BF16)

    p = _dot(h_scr[...], w_ref[...])
    if n_rope == 0:
        o_ref[...] = p
    else:
        @pl.when(j < n_rope)
        def _():
            c = cos_ref[...]
            s = sin_ref[...]
            for h in range(p.shape[1] // A_HEAD_DIM):
                hs = slice(h * A_HEAD_DIM, (h + 1) * A_HEAD_DIM)
                ph = p[:, hs]
                o_ref[:, hs] = ph * c + pltpu.roll(ph, A_HEAD_DIM // 2, 1) * s

        @pl.when(j >= n_rope)
        def _():
            o_ref[...] = p


def _norm_inproj(x, g, w_bf16, cos_t, sin_t, *, n_rope, tm):
    t, d = x.shape
    n = w_bf16.shape[1]
    tn = 1024
    n_pos_tiles = cos_t.shape[0] // tm
    return pl.pallas_call(
        functools.partial(_inproj_kernel, n_rope=n_rope),
        grid=(t // tm, n // tn),
        in_specs=[
            pl.BlockSpec((tm, d), lambda i, j: (i, 0)),
            pl.BlockSpec((1, d), lambda i, j: (0, 0)),
            pl.BlockSpec((d, tn), lambda i, j: (0, j)),
            pl.BlockSpec((tm, LANES), lambda i, j: (i % n_pos_tiles, 0)),
            pl.BlockSpec((tm, LANES), lambda i, j: (i % n_pos_tiles, 0)),
        ],
        out_specs=pl.BlockSpec((tm, tn), lambda i, j: (i, j)),
        out_shape=jax.ShapeDtypeStruct((t, n), F32),
        scratch_shapes=[pltpu.VMEM((tm, d), BF16)],
        compiler_params=_cparams("parallel", "arbitrary"),
        name="norm_inproj",
    )(x, g.reshape(1, d), w_bf16, cos_t, sin_t)


def _attn_kernel(q_ref, kc_ref, kp_ref, vc_ref, vp_ref, o_ref, lse_ref, *, tq):
    ub = pl.program_id(2)
    blk = B_CHUNK
    nsub = tq // blk
    row = lax.broadcasted_iota(jnp.int32, (blk, 2 * blk), 0)
    col = lax.broadcasted_iota(jnp.int32, (blk, 2 * blk), 1)
    dist = jnp.where(col >= row, col - row, 2 * blk)
    band = dist <= blk
    first_valid = jnp.where(ub > 0, 0, blk)
    band0 = jnp.where(col >= first_valid, dist, 2 * blk) <= blk
    lane_grp = lax.broadcasted_iota(jnp.int32, (blk, LANES), 1) // (LANES // A_HEADS)
    scale = A_HEAD_DIM ** -0.5
    for s in range(nsub):
        rows = slice(s * blk, (s + 1) * blk)
        lse_tile = jnp.zeros((blk, LANES), F32)
        for h in range(A_HEADS):
            hs = slice(h * A_HEAD_DIM, (h + 1) * A_HEAD_DIM)
            q = q_ref[0, rows, hs].astype(BF16)
            if s == 0:
                k = jnp.concatenate([kp_ref[0, :, hs], kc_ref[0, 0:blk, hs]], axis=0)
                v = jnp.concatenate([vp_ref[0, :, hs], vc_ref[0, 0:blk, hs]], axis=0)
                mask = band0
            else:
                k = kc_ref[0, (s - 1) * blk:(s + 1) * blk, hs]
                v = vc_ref[0, (s - 1) * blk:(s + 1) * blk, hs]
                mask = band
            sc = _dot_nt(q, k.astype(BF16)) * scale
            sc = jnp.where(mask, sc, -jnp.inf)
            m = jnp.max(sc, axis=-1, keepdims=True)
            p = jnp.exp(sc - m)
            l = jnp.sum(p, axis=-1, keepdims=True)
            o_ref[0, rows, hs] = _dot(p.astype(BF16), v.astype(BF16)) / l
            lse_tile = jnp.where(lane_grp == h, m + jnp.log(l), lse_tile)
        lse_ref[0, rows, :] = lse_tile


def _attn_pattern(proj, dil, bsz, seq):
    lu = seq // dil
    tq = min(512, lu)
    nsub = tq // B_CHUNK
    view = proj.reshape(bsz, lu, dil * AB_IN)
    ncb = AB_IN // A_WIDTH
    cur = lambda c: pl.BlockSpec((1, tq, A_WIDTH), lambda b, r, u: (b, u, ncb * r + c))
    prev = lambda c: pl.BlockSpec((1, B_CHUNK, A_WIDTH),
                                  lambda b, r, u: (b, jnp.maximum(u * nsub - 1, 0), ncb * r + c))
    o, lse = pl.pallas_call(
        functools.partial(_attn_kernel, tq=tq),
        grid=(bsz, dil, lu // tq),
        in_specs=[cur(0), cur(1), prev(1), cur(2), prev(2)],
        out_specs=[pl.BlockSpec((1, tq, A_WIDTH), lambda b, r, u: (b, u, r)),
                   pl.BlockSpec((1, tq, LANES), lambda b, r, u: (b, u, r))],
        out_shape=[jax.ShapeDtypeStruct((bsz, lu, dil * A_WIDTH), F32),
                   jax.ShapeDtypeStruct((bsz, lu, dil * LANES), F32)],
        compiler_params=_cparams("parallel", "parallel", "arbitrary"),
        name=f"dilated_attn_d{dil}",
    )(view, view, view, view, view)
    return o.reshape(bsz * seq, A_WIDTH), lse.reshape(bsz * seq, LANES)


def _decode_attn_kernel(q_ref, kn_ref, vn_ref, k1_ref, k4_ref, k16_ref, v1_ref, v4_ref, v16_ref, o_ref, *, nb):
    scale = A_HEAD_DIM ** -0.5
    for bb in range(nb):
        q = q_ref[bb]
        kn = kn_ref[bb]
        vn = vn_ref[bb]
        s_new = jnp.sum(q * kn, axis=-1, keepdims=True) * scale
        outs, lses = [], []
        for k_ref, v_ref in ((k1_ref, v1_ref), (k4_ref, v4_ref), (k16_ref, v16_ref)):
            kk = k_ref[0, bb, :, 0]
            vv = v_ref[0, bb, :, 0]
            sc = jnp.sum(kk * q[None], axis=-1, keepdims=True) * scale
            m = jnp.maximum(jnp.max(sc, axis=0), s_new)
            p = jnp.exp(sc - m[None])
            p_new = jnp.exp(s_new - m)
            l = jnp.sum(p, axis=0) + p_new
            outs.append((jnp.sum(p * vv, axis=0) + p_new * vn) / l)
            lses.append(m + jnp.log(l))
        mx = jnp.maximum(jnp.maximum(lses[0], lses[1]), lses[2])
        e = [jnp.exp(x - mx) for x in lses]
        tot = e[0] + e[1] + e[2]
        o_ref[bb] = (outs[0] * e[0] + outs[1] * e[1] + outs[2] * e[2]) / tot


def _decode_attn(qkv, cache_k, cache_v, li):
    bd = qkv.shape[0]
    n_layers, _, win = cache_k.shape[:3]
    nb = 4
    new = lambda c: pl.BlockSpec((nb, A_HEADS, A_HEAD_DIM), lambda i: (i, c, 0))
    specs, views = [], []
    for cache in (cache_k, cache_v):
        for window, dil in A_PATTERNS:
            rows = win // dil
            n_keys = window // dil
            views.append(cache.reshape(n_layers, bd, rows, dil, A_HEADS, A_HEAD_DIM))
            specs.append(pl.BlockSpec((1, nb, n_keys, 1, A_HEADS, A_HEAD_DIM),
                                      lambda i, rb=rows // n_keys - 1: (li, i, rb, 0, 0, 0)))
    return pl.pallas_call(
        functools.partial(_decode_attn_kernel, nb=nb),
        grid=(bd // nb,),
        in_specs=[new(0), new(1), new(2)] + specs,
        out_specs=pl.BlockSpec((nb, A_HEADS, A_HEAD_DIM), lambda i: (i, 0, 0)),
        out_shape=jax.ShapeDtypeStruct((bd, A_HEADS, A_HEAD_DIM), F32),
        compiler_params=_cparams("parallel"),
        name="decode_attn",
    )(qkv, qkv, qkv, *views)


_SHIFTS = tuple(range(-(B_QKV_BLOCK - 1), B_QKV_BLOCK))


def _blockdiag4_weights(w):
    nb = w.shape[0]
    rows = []
    for s in _SHIFTS:
        cols = []
        for o in range(B_QKV_BLOCK):
            i = o + s
            cols.append(w[:, i, o] if 0 <= i < B_QKV_BLOCK else jnp.zeros((nb,), w.dtype))
        rows.append(jnp.stack(cols, axis=1).reshape(nb * B_QKV_BLOCK))
    return jnp.stack(rows, axis=0)


def _lane_shifts(x):
    w = x.shape[1]
    out = []
    for s in _SHIFTS:
        if s == 0:
            out.append(x)
        else:
            k = (-s) % LANES
            out.append(jnp.concatenate(
                [pltpu.roll(x[:, g * LANES:(g + 1) * LANES], k, 1) for g in range(w // LANES)], axis=1))
    return out


def _blockdiag4(shifted, w7_ref):
    acc = shifted[0] * w7_ref[0:1, :]
    for i in range(1, len(_SHIFTS)):
        acc = acc + shifted[i] * w7_ref[i:i + 1, :]
    return acc


def _mlstm_qkv_gates(xb, xc, wq7_ref, wk7_ref, wv7_ref, wg_ref, bg_ref):
    sh_c = _lane_shifts(xc)
    qb = _blockdiag4(sh_c, wq7_ref)
    kb = _blockdiag4(sh_c, wk7_ref)
    vb = _blockdiag4(_lane_shifts(xb), wv7_ref)
    qh, kh, vh = qb.astype(BF16), kb.astype(BF16), vb.astype(BF16)
    g = (_dot(qh, wg_ref[0:B_WIDTH, :]) + _dot(kh, wg_ref[B_WIDTH:2 * B_WIDTH, :])
         + _dot(vh, wg_ref[2 * B_WIDTH:3 * B_WIDTH, :]) + bg_ref[...])
    return qb, kb, vb, qh, kh, vh, g


def _cumsum_rows(x):
    t = x.shape[0]
    idx = lax.broadcasted_iota(jnp.int32, x.shape, 0)
    k = 1
    while k < t:
        x = x + jnp.where(idx >= k, pltpu.roll(x, k, 0), 0.0)
        k *= 2
    return x


def _cumsum_lanes(x):
    t = x.shape[1]
    idx = lax.broadcasted_iota(jnp.int32, x.shape, 1)
    k = 1
    while k < t:
        x = x + jnp.where(idx >= k, pltpu.roll(x, k, 1), 0.0)
        k *= 2
    return x


def _mlstm_prompt_kernel(xb_ref, cw_ref, cb_ref, wq7_ref, wk7_ref, wv7_ref, wg_ref, wgt_ref, bg_ref, bgt_ref,
                         hb_ref, xc_ref, c_out, n_out, m_out, buf_out,
                         xx_scr, c_scr, n_scr, m_scr):
    ci = pl.program_id(1)
    nc = pl.num_programs(1)
    t = B_CHUNK
    halo = SUBLANES

    @pl.when(ci == 0)
    def _():
        xx_scr[0:halo, :] = jnp.zeros((halo, B_WIDTH), F32)
        c_scr[...] = jnp.zeros_like(c_scr)
        n_scr[...] = jnp.zeros_like(n_scr)
        m_scr[...] = jnp.zeros_like(m_scr)

    x = xb_ref[0]
    xx_scr[halo:halo + t, :] = x
    xconv = x * cw_ref[3:4, :] + cb_ref[...]
    for s in range(1, CONV_WIDTH):
        xconv = xconv + xx_scr[pl.ds(halo - s, t), :] * cw_ref[3 - s:4 - s, :]
    xx_scr[0:halo, :] = x[t - halo:t, :]
    xc = _silu(xconv)
    xc_ref[0] = xc

    qb, kb, vb, qh, kh, vh, g = _mlstm_qkv_gates(x, xc, wq7_ref, wk7_ref, wv7_ref, wg_ref, bg_ref)
    gt = (_dot_nt(wgt_ref[:, 0:B_WIDTH], qh) + _dot_nt(wgt_ref[:, B_WIDTH:2 * B_WIDTH], kh)
          + _dot_nt(wgt_ref[:, 2 * B_WIDTH:3 * B_WIDTH], vh) + bgt_ref[...])
    bcol = _cumsum_rows(_log_sigmoid(g))
    brow = _cumsum_lanes(_log_sigmoid(gt))
    ri = lax.broadcasted_iota(jnp.int32, (t, t), 0)
    cj = lax.broadcasted_iota(jnp.int32, (t, t), 1)
    causal = cj <= ri
    kscale = B_HEAD_DIM ** -0.5
    for h in range(B_HEADS):
        sl = slice(h * B_HEAD_DIM, (h + 1) * B_HEAD_DIM)
        q = qb[:, sl]
        k = kb[:, sl] * kscale
        qhh = qh[:, sl]
        khh = k.astype(BF16)
        vhh = vh[:, sl]
        logi_c = g[:, h:h + 1]
        b_c = bcol[:, B_HEADS + h:B_HEADS + h + 1]
        logi_r = gt[h:h + 1, :]
        b_r = brow[B_HEADS + h:B_HEADS + h + 1, :]
        m0 = m_scr[h:h + 1, 0:1]
        n0 = n_scr[h:h + 1, :]
        c0 = c_scr[h]
        dmat = jnp.where(causal, b_c - b_r + logi_r, -jnp.inf)
        inter = b_c + m0
        m = jnp.maximum(jnp.max(dmat, axis=1, keepdims=True), inter)
        w_intra = jnp.exp(dmat - m)
        w_inter = jnp.exp(inter - m)
        s = _dot_nt(qhh, khh) * w_intra
        num = _dot(s.astype(BF16), vhh) + w_inter * _dot(qhh, c0.astype(BF16))
        den = jnp.sum(s, axis=1, keepdims=True) + w_inter * jnp.sum(q * n0, axis=1, keepdims=True)
        hb_ref[0, :, sl] = num / jnp.maximum(jnp.abs(den), jnp.exp(-m))
        bl = b_c[t - 1:t, :]
        dl = bl - b_c + logi_c
        m_new = jnp.maximum(bl + m0, jnp.max(dl, axis=0, keepdims=True))
        kw = jnp.exp(dl - m_new) * k
        carry = jnp.exp(bl + m0 - m_new)
        c_scr[h] = carry * c0 + _dot_tn(kw.astype(BF16), vhh)
        n_scr[h:h + 1, :] = carry * n0 + jnp.sum(kw, axis=0, keepdims=True)
        m_scr[h:h + 1, :] = jnp.broadcast_to(m_new, (1, LANES))

    @pl.when(ci == nc - 1)
    def _():
        c_out[0] = c_scr[...]
        n_out[0] = n_scr[0:B_HEADS, :]
        lane = lax.broadcasted_iota(jnp.int32, (1, LANES), 1)
        mt = jnp.zeros((1, LANES), F32)
        for h in range(B_HEADS):
            mt = jnp.where(lane == h, m_scr[h:h + 1, :], mt)
        m_out[0] = mt
        buf_out[0] = x[t - halo:t, :]


def _mlstm_prompt(proj, lw, bsz, seq):
    t = B_CHUNK
    view = proj.reshape(bsz, seq, AB_IN)
    full = lambda a: pl.BlockSpec(a.shape, lambda b, c: (0,) * a.ndim)
    params = [lw["conv_w"], lw["conv_b"], lw["wq7"], lw["wk7"], lw["wv7"], lw["wg"], lw["wgt"], lw["bg"], lw["bgt"]]
    hb, xc, c, n, m, buf = pl.pallas_call(
        _mlstm_prompt_kernel,
        grid=(bsz, seq // t),
        in_specs=[pl.BlockSpec((1, t, B_WIDTH), lambda b, c: (b, c, 4))] + [full(a) for a in params],
        out_specs=[pl.BlockSpec((1, t, B_WIDTH), lambda b, c: (b, c, 0)),
                   pl.BlockSpec((1, t, B_WIDTH), lambda b, c: (b, c, 0)),
                   pl.BlockSpec((1, B_HEADS, B_HEAD_DIM, B_HEAD_DIM), lambda b, c: (b, 0, 0, 0)),
                   pl.BlockSpec((1, B_HEADS, B_HEAD_DIM), lambda b, c: (b, 0, 0)),
                   pl.BlockSpec((1, 1, LANES), lambda b, c: (b, 0, 0)),
                   pl.BlockSpec((1, SUBLANES, B_WIDTH), lambda b, c: (b, 0, 0))],
        out_shape=[jax.ShapeDtypeStruct((bsz, seq, B_WIDTH), F32),
                   jax.ShapeDtypeStruct((bsz, seq, B_WIDTH), F32),
                   jax.ShapeDtypeStruct((bsz, B_HEADS, B_HEAD_DIM, B_HEAD_DIM), F32),
                   jax.ShapeDtypeStruct((bsz, B_HEADS, B_HEAD_DIM), F32),
                   jax.ShapeDtypeStruct((bsz, 1, LANES), F32),
                   jax.ShapeDtypeStruct((bsz, SUBLANES, B_WIDTH), F32)],
        scratch_shapes=[pltpu.VMEM((t + SUBLANES, B_WIDTH), F32),
                        pltpu.VMEM((B_HEADS, B_HEAD_DIM, B_HEAD_DIM), F32),
                        pltpu.VMEM((SUBLANES, B_HEAD_DIM), F32),
                        pltpu.VMEM((SUBLANES, LANES), F32)],
        compiler_params=_cparams("parallel", "arbitrary"),
        name="mlstm_prompt",
    )(view, *params)
    return (hb.reshape(bsz * seq, B_WIDTH), xc.reshape(bsz * seq, B_WIDTH), c, n,
            m[:, 0, :B_HEADS], buf[:, SUBLANES - (CONV_WIDTH - 1):, :])


def _mlstm_decode_pre_kernel(xb_ref, b0_ref, b1_ref, b2_ref, cw_ref, cb_ref, wq7_ref, wk7_ref, wv7_ref, wg_ref,
                             bg_ref, n0_ref, m0_ref,
                             xc_ref, q_ref, v_ref, kw_ref, sv_ref, wi_ref, den_ref, n_out, m_out, buf_out,
                             qt_ref, kwt_ref):
    x = xb_ref[...]
    xconv = (x * cw_ref[3:4, :] + b2_ref[...] * cw_ref[2:3, :] + b1_ref[...] * cw_ref[1:2, :]
             + b0_ref[...] * cw_ref[0:1, :] + cb_ref[...])
    xc = _silu(xconv)
    xc_ref[...] = xc
    buf_out[:, 0:B_WIDTH] = b1_ref[...]
    buf_out[:, B_WIDTH:2 * B_WIDTH] = b2_ref[...]
    buf_out[:, 2 * B_WIDTH:3 * B_WIDTH] = x
    qb, kb, vb, _, _, _, g = _mlstm_qkv_gates(x, xc, wq7_ref, wk7_ref, wv7_ref, wg_ref, bg_ref)
    q_ref[...] = qb
    v_ref[...] = vb
    logf = _log_sigmoid(g)
    rows = x.shape[0]
    lane = lax.broadcasted_iota(jnp.int32, (rows, LANES), 1)
    m_tile = jnp.zeros((rows, LANES), F32)
    kscale = B_HEAD_DIM ** -0.5
    for h in range(B_HEADS):
        sl = slice(h * B_HEAD_DIM, (h + 1) * B_HEAD_DIM)
        logi = g[:, h:h + 1]
        fm = logf[:, B_HEADS + h:B_HEADS + h + 1] + m0_ref[:, h:h + 1]
        m = jnp.maximum(logi, fm)
        w_intra = jnp.exp(logi - m)
        w_inter = jnp.exp(fm - m)
        q = qb[:, sl]
        k = kb[:, sl] * kscale
        n0 = n0_ref[:, sl]
        s = jnp.sum(q * k, axis=1, keepdims=True) * w_intra
        den = s + w_inter * jnp.sum(q * n0, axis=1, keepdims=True)
        kw = w_intra * k
        kw_ref[:, sl] = kw
        sv_ref[:, sl] = s * vb[:, sl]
        wi_ref[:, sl] = jnp.broadcast_to(w_inter, q.shape)
        den_ref[:, sl] = jnp.broadcast_to(jnp.maximum(jnp.abs(den), jnp.exp(-m)), q.shape)
        n_out[:, sl] = w_inter * n0 + kw
        m_tile = jnp.where(lane == h, m, m_tile)
        qt_ref[h] = q.T
        kwt_ref[h] = kw.T
    m_out[...] = m_tile


def _mlstm_decode_state_kernel(c0_ref, qt_ref, kwt_ref, v_ref, wi_ref, c_out, qc_ref, *, nb):
    i = pl.program_id(1)
    lane = lax.broadcasted_iota(jnp.int32, qt_ref.shape[1:], 1)
    qt = qt_ref[0]
    kwt = kwt_ref[0]
    for bb in range(nb):
        sel = lane == i * nb + bb
        qcol = jnp.sum(jnp.where(sel, qt, 0.0), axis=1, keepdims=True)
        kcol = jnp.sum(jnp.where(sel, kwt, 0.0), axis=1, keepdims=True)
        c0 = c0_ref[bb, 0]
        qc_ref[bb] = jnp.sum(c0 * qcol, axis=0, keepdims=True)
        c_out[bb, 0] = wi_ref[bb] * c0 + kcol * v_ref[bb]


def _mlstm_decode(proj, conv_state, c0, n0, m0, lw):
    bd = proj.shape[0]
    w = B_WIDTH
    cs = conv_state.reshape(bd, (CONV_WIDTH - 1) * w)
    m0p = jnp.pad(m0, ((0, 0), (0, LANES - B_HEADS)))
    row = lambda c: pl.BlockSpec((bd, w), lambda i: (0, c))
    full = lambda a: pl.BlockSpec(a.shape, lambda i: (0,) * a.ndim)
    params = [lw["conv_w"], lw["conv_b"], lw["wq7"], lw["wk7"], lw["wv7"], lw["wg"], lw["bg"]]
    n0f = n0.reshape(bd, w)
    sds = jax.ShapeDtypeStruct
    outs = pl.pallas_call(
        _mlstm_decode_pre_kernel,
        grid=(1,),
        in_specs=[row(4), row(0), row(1), row(2)] + [full(a) for a in params] + [full(n0f), full(m0p)],
        out_specs=[row(0)] * 8 + [pl.BlockSpec((bd, LANES), lambda i: (0, 0)),
                                  pl.BlockSpec((bd, 3 * w), lambda i: (0, 0)),
                                  pl.BlockSpec((B_HEADS, B_HEAD_DIM, bd), lambda i: (0, 0, 0)),
                                  pl.BlockSpec((B_HEADS, B_HEAD_DIM, bd), lambda i: (0, 0, 0))],
        out_shape=[sds((bd, w), F32)] * 8 + [sds((bd, LANES), F32), sds((bd, 3 * w), F32),
                                             sds((B_HEADS, B_HEAD_DIM, bd), F32),
                                             sds((B_HEADS, B_HEAD_DIM, bd), F32)],
        compiler_params=_cparams("arbitrary"),
        name="mlstm_decode_pre",
    )(proj, cs, cs, cs, *params, n0f, m0p)
    xc, q, v, kw, sv, wi, den, n_new, m_new, buf, qt, kwt = outs
    nb = 8
    rowh = pl.BlockSpec((nb, 1, B_HEAD_DIM), lambda h, i: (i, 0, h))
    colh = pl.BlockSpec((1, B_HEAD_DIM, bd), lambda h, i: (h, 0, 0))
    cblk = pl.BlockSpec((nb, 1, B_HEAD_DIM, B_HEAD_DIM), lambda h, i: (i, h, 0, 0))
    c_new, qc = pl.pallas_call(
        functools.partial(_mlstm_decode_state_kernel, nb=nb),
        grid=(B_HEADS, bd // nb),
        in_specs=[cblk, colh, colh, rowh, rowh],
        out_specs=[cblk, rowh],
        out_shape=[sds(c0.shape, F32), sds((bd, 1, w), F32)],
        compiler_params=_cparams("parallel", "arbitrary"),
        name="mlstm_decode_state",
    )(c0, qt, kwt, v.reshape(bd, 1, w), wi.reshape(bd, 1, w))
    return (dict(xc=xc, sv=sv, wi=wi, den=den, qc=qc.reshape(bd, w)), c_new,
            n_new.reshape(bd, B_HEADS, B_HEAD_DIM), m_new[:, :B_HEADS], buf.reshape(bd, CONV_WIDTH - 1, w))


def _ab_out_kernel(*refs, decode):
    if decode:
        (oa_ref, sv_ref, wi_ref, qc_ref, den_ref, za_ref, zb_ref, xc_ref, x_ref,
         w_ref, on_ref, sk_ref, g_ref, o_ref) = refs
        oa = oa_ref[...]
        hb_raw = (sv_ref[...] + wi_ref[...] * qc_ref[...]) / den_ref[...]
    else:
        (o1_ref, o4_ref, o16_ref, l1_ref, l4_ref, l16_ref, hb_ref, za_ref, zb_ref, xc_ref, x_ref,
         w_ref, on_ref, sk_ref, g_ref, o_ref) = refs
        l1, l4, l16 = l1_ref[...], l4_ref[...], l16_ref[...]
        mx = jnp.maximum(jnp.maximum(l1, l4), l16)
        e1, e4, e16 = jnp.exp(l1 - mx), jnp.exp(l4 - mx), jnp.exp(l16 - mx)
        tot = e1 + e4 + e16
        w1, w4, w16 = e1 / tot, e4 / tot, e16 / tot
        grp = LANES // A_HEADS
        parts = []
        for h in range(A_HEADS):
            hs = slice(h * A_HEAD_DIM, (h + 1) * A_HEAD_DIM)
            c = slice(h * grp, h * grp + 1)
            parts.append(o1_ref[:, hs] * w1[:, c] + o4_ref[:, hs] * w4[:, c] + o16_ref[:, hs] * w16[:, c])
        oa = jnp.concatenate(parts, axis=1)
        hb_raw = hb_ref[...]
    ya = oa * _silu(za_ref[...])
    hparts = []
    for h in range(B_HEADS):
        sl = slice(h * B_HEAD_DIM, (h + 1) * B_HEAD_DIM)
        xh = hb_raw[:, sl]
        xcn = xh - jnp.mean(xh, axis=-1, keepdims=True)
        hparts.append(xcn * lax.rsqrt(jnp.mean(xcn * xcn, axis=-1, keepdims=True) + 1e-5))
    hb = jnp.concatenate(hparts, axis=1) * on_ref[...] + sk_ref[...] * xc_ref[...]
    yb = hb * _silu(zb_ref[...])
    y = _dot(ya.astype(BF16), w_ref[0:A_WIDTH, :]) + _dot(yb.astype(BF16), w_ref[A_WIDTH:A_WIDTH + B_WIDTH, :])
    o_ref[...] = x_ref[...] + _rmsnorm(y, g_ref[...])


def _ab_out(x, proj, lw, *, tm, attn=None, mlstm=None, decode_in=None):
    t = x.shape[0]
    w = D_MODEL
    row = pl.BlockSpec((tm, w), lambda i: (i, 0))
    lrow = pl.BlockSpec((tm, LANES), lambda i: (i, 0))
    pcol = lambda c: pl.BlockSpec((tm, w), lambda i: (i, c))
    full = lambda a: pl.BlockSpec(a.shape, lambda i: (0,) * a.ndim)
    params = [lw["w_out"], lw["out_norm"], lw["skip"], lw["norm_post"]]
    if decode_in is not None:
        d = decode_in
        ins = [d["oa"], d["sv"], d["wi"], d["qc"], d["den"], proj, proj, d["xc"], x]
        specs = [row] * 5 + [pcol(3), pcol(5), row, row]
    else:
        (o1, l1), (o4, l4), (o16, l16) = attn
        hb, xc = mlstm
        ins = [o1, o4, o16, l1, l4, l16, hb, proj, proj, xc, x]
        specs = [row] * 3 + [lrow] * 3 + [row, pcol(3), pcol(5), row, row]
    return pl.pallas_call(
        functools.partial(_ab_out_kernel, decode=decode_in is not None),
        grid=(t // tm,),
        in_specs=specs + [full(a) for a in params],
        out_specs=row,
        out_shape=jax.ShapeDtypeStruct((t, w), F32),
        compiler_params=_cparams("parallel"),
        name="ab_out",
    )(*ins, *params)


def _rglru_gates(xconv, wa_ref, ba_ref, wx_ref, bx_ref, lam_ref):
    rs, is_ = [], []
    for n in range(C_BLOCKS):
        sl = slice(n * C_BLOCK, (n + 1) * C_BLOCK)
        xa = xconv[:, sl].astype(BF16)
        rs.append(_dot(xa, wa_ref[n]))
        is_.append(_dot(xa, wx_ref[n]))
    r = _sigmoid(jnp.concatenate(rs, axis=1) + ba_ref[...])
    ig = _sigmoid(jnp.concatenate(is_, axis=1) + bx_ref[...])
    log_a = -RG_LRU_C * r * _softplus(-lam_ref[...])
    a = jnp.exp(log_a)
    u = jnp.sqrt(1.0 - a * a) * (ig * xconv)
    return a, u


def _rglru_out(h, z, x, w_ref, g_ref):
    y = _dot((h * _silu(z)).astype(BF16), w_ref[...])
    return x + _rmsnorm(y, g_ref[...])


def _rglru_prompt_kernel(xc_ref, zc_ref, x_ref, cw_ref, cb_ref, wa_ref, ba_ref, wx_ref, bx_ref, lam_ref,
                         w_ref, g_ref, o_ref, h_out, buf_out, xx_scr, a_scr, u_scr, hc_scr, *, tm):
    ti = pl.program_id(1)
    nt = pl.num_programs(1)
    halo = SUBLANES

    @pl.when(ti == 0)
    def _():
        xx_scr[0:halo, :] = jnp.zeros((halo, C_WIDTH), F32)
        hc_scr[...] = jnp.zeros_like(hc_scr)

    x = xc_ref[0]
    xx_scr[halo:halo + tm, :] = x
    xconv = x * cw_ref[3:4, :] + cb_ref[...]
    for s in range(1, CONV_WIDTH):
        xconv = xconv + xx_scr[pl.ds(halo - s, tm), :] * cw_ref[3 - s:4 - s, :]
    xx_scr[0:halo, :] = x[tm - halo:tm, :]
    a, u = _rglru_gates(xconv, wa_ref, ba_ref, wx_ref, bx_ref, lam_ref)
    a_scr[...] = a
    u_scr[...] = u
    ridx = lax.broadcasted_iota(jnp.int32, (SUBLANES, C_WIDTH), 0)

    def body(r, hprev):
        r8 = pl.multiple_of(r * SUBLANES, SUBLANES)
        aa = a_scr[pl.ds(r8, SUBLANES), :]
        uu = u_scr[pl.ds(r8, SUBLANES), :]
        for sft in (1, 2, 4):
            keep = ridx >= sft
            a_sh = jnp.where(keep, pltpu.roll(aa, sft, 0), 1.0)
            u_sh = jnp.where(keep, pltpu.roll(uu, sft, 0), 0.0)
            uu = aa * u_sh + uu
            aa = aa * a_sh
        hblk = aa * hprev + uu
        u_scr[pl.ds(r8, SUBLANES), :] = hblk
        return hblk[SUBLANES - 1:SUBLANES, :]

    hlast = lax.fori_loop(0, tm // SUBLANES, body, hc_scr[0:1, :])
    hc_scr[0:1, :] = hlast
    o_ref[0] = _rglru_out(u_scr[...], zc_ref[0], x_ref[0], w_ref, g_ref)

    @pl.when(ti == nt - 1)
    def _():
        h_out[0] = hlast
        buf_out[0] = x[tm - halo:tm, :]


def _rglru_decode_kernel(xc_ref, zc_ref, x_ref, b0_ref, b1_ref, b2_ref, h0_ref, cw_ref, cb_ref, wa_ref, ba_ref,
                         wx_ref, bx_ref, lam_ref, w_ref, g_ref, o_ref, h_out, buf_out):
    x = xc_ref[...]
    xconv = (x * cw_ref[3:4, :] + b2_ref[...] * cw_ref[2:3, :] + b1_ref[...] * cw_ref[1:2, :]
             + b0_ref[...] * cw_ref[0:1, :] + cb_ref[...])
    a, u = _rglru_gates(xconv, wa_ref, ba_ref, wx_ref, bx_ref, lam_ref)
    h = a * h0_ref[...] + u
    h_out[...] = h
    buf_out[:, 0:C_WIDTH] = b1_ref[...]
    buf_out[:, C_WIDTH:2 * C_WIDTH] = b2_ref[...]
    buf_out[:, 2 * C_WIDTH:3 * C_WIDTH] = x
    o_ref[...] = _rglru_out(h, zc_ref[...], x_ref[...], w_ref, g_ref)


def _c_params(lw):
    return [lw["conv_w"], lw["conv_b"], lw["w_a"], lw["b_a"], lw["w_x"], lw["b_x"], lw["lam"],
            lw["w_out"], lw["norm_post"]]


def _rglru_prompt(x, proj, lw, bsz, seq, tm):
    cw = C_WIDTH
    pv = proj.reshape(bsz, seq, C_IN)
    xv = x.reshape(bsz, seq, D_MODEL)
    params = _c_params(lw)
    full = lambda a: pl.BlockSpec(a.shape, lambda b, i: (0,) * a.ndim)
    sds = jax.ShapeDtypeStruct
    y, h_last, buf = pl.pallas_call(
        functools.partial(_rglru_prompt_kernel, tm=tm),
        grid=(bsz, seq // tm),
        in_specs=[pl.BlockSpec((1, tm, cw), lambda b, i: (b, i, 0)),
                  pl.BlockSpec((1, tm, cw), lambda b, i: (b, i, 1)),
                  pl.BlockSpec((1, tm, D_MODEL), lambda b, i: (b, i, 0))] + [full(a) for a in params],
        out_specs=[pl.BlockSpec((1, tm, D_MODEL), lambda b, i: (b, i, 0)),
                   pl.BlockSpec((1, 1, cw), lambda b, i: (b, 0, 0)),
                   pl.BlockSpec((1, SUBLANES, cw), lambda b, i: (b, 0, 0))],
        out_shape=[sds((bsz, seq, D_MODEL), F32), sds((bsz, 1, cw), F32), sds((bsz, SUBLANES, cw), F32)],
        scratch_shapes=[pltpu.VMEM((tm + SUBLANES, cw), F32), pltpu.VMEM((tm, cw), F32),
                        pltpu.VMEM((tm, cw), F32), pltpu.VMEM((SUBLANES, cw), F32)],
        compiler_params=_cparams("parallel", "arbitrary"),
        name="rglru_prompt",
    )(pv, pv, xv, *params)
    return y.reshape(bsz * seq, D_MODEL), h_last[:, 0, :], buf[:, SUBLANES - (CONV_WIDTH - 1):, :]


def _rglru_decode(x, proj, conv_state, h0, lw):
    bd = x.shape[0]
    cw = C_WIDTH
    cs = conv_state.reshape(bd, (CONV_WIDTH - 1) * cw)
    params = _c_params(lw)
    col = lambda c: pl.BlockSpec((bd, cw), lambda i: (0, c))
    full = lambda a: pl.BlockSpec(a.shape, lambda i: (0,) * a.ndim)
    sds = jax.ShapeDtypeStruct
    y, h, buf = pl.pallas_call(
        _rglru_decode_kernel,
        grid=(1,),
        in_specs=[col(0), col(1), full(x), col(0), col(1), col(2), full(h0)] + [full(a) for a in params],
        out_specs=[pl.BlockSpec((bd, D_MODEL), lambda i: (0, 0)), pl.BlockSpec((bd, cw), lambda i: (0, 0)),
                   pl.BlockSpec((bd, 3 * cw), lambda i: (0, 0))],
        out_shape=[sds((bd, D_MODEL), F32), sds((bd, cw), F32), sds((bd, 3 * cw), F32)],
        compiler_params=_cparams("arbitrary"),
        name="rglru_decode",
    )(proj, proj, x, cs, cs, cs, h0, *params)
    return y, h, buf.reshape(bd, CONV_WIDTH - 1, cw)


def _rope_tables(pos):
    half = A_HEAD_DIM // 2
    freqs = ROPE_THETA ** (-jnp.arange(half, dtype=F32) / half)
    ang = pos.astype(F32)[:, None] * freqs[None, :]
    cos, sin = jnp.cos(ang), jnp.sin(ang)
    return jnp.concatenate([cos, cos], axis=1), jnp.concatenate([-sin, sin], axis=1)


def _prep_ab(li, ab_norm_pre, ab_norm_post, ab_w_in, ab_w_out, mlstm_conv_w, mlstm_conv_b, mlstm_wq, mlstm_wk,
             mlstm_wv, mlstm_w_gates, mlstm_b_gates, mlstm_out_norm, mlstm_skip):
    ng = 2 * B_HEADS
    wg = jnp.pad(mlstm_w_gates[li], ((0, 0), (0, LANES - ng))).astype(BF16)
    return dict(
        norm_pre=ab_norm_pre[li], norm_post=ab_norm_post[li].reshape(1, D_MODEL),
        w_in=ab_w_in[li].astype(BF16), w_out=ab_w_out[li].astype(BF16),
        conv_w=mlstm_conv_w[li], conv_b=mlstm_conv_b[li].reshape(1, B_WIDTH),
        wq7=_blockdiag4_weights(mlstm_wq[li]), wk7=_blockdiag4_weights(mlstm_wk[li]),
        wv7=_blockdiag4_weights(mlstm_wv[li]),
        wg=wg, wgt=mlstm_w_gates[li].T.astype(BF16),
        bg=jnp.pad(mlstm_b_gates[li], (0, LANES - ng)).reshape(1, LANES),
        bgt=mlstm_b_gates[li].reshape(ng, 1),
        out_norm=mlstm_out_norm[li].reshape(1, B_WIDTH), skip=mlstm_skip[li].reshape(1, B_WIDTH))


def _prep_c(li, c_norm_pre, c_norm_post, c_w_in, c_w_out, c_conv_w, c_conv_b, c_w_a, c_b_a, c_w_x, c_b_x,
            c_lambda):
    return dict(
        norm_pre=c_norm_pre[li], norm_post=c_norm_post[li].reshape(1, D_MODEL),
        w_in=c_w_in[li].astype(BF16), w_out=c_w_out[li].astype(BF16),
        conv_w=c_conv_w[li], conv_b=c_conv_b[li].reshape(1, C_WIDTH),
        w_a=c_w_a[li].astype(BF16), b_a=c_b_a[li].reshape(1, C_WIDTH),
        w_x=c_w_x[li].astype(BF16), b_x=c_b_x[li].reshape(1, C_WIDTH),
        lam=c_lambda[li].reshape(1, C_WIDTH))


def _prompt_trunk(x_prompt, ab_layers, c_layers):
    bsz, seq, _ = x_prompt.shape
    x = x_prompt.reshape(bsz * seq, D_MODEL)
    tm_in = 512
    cos_t, sin_t = _rope_tables(jnp.arange(seq, dtype=jnp.int32))
    outs = [[] for _ in range(8)]
    keep = min(A_MAX_WINDOW, seq)
    for li in range(len(ab_layers)):
        lw = ab_layers[li]
        proj = _norm_inproj(x, lw["norm_pre"], lw["w_in"], cos_t, sin_t, n_rope=2, tm=tm_in)
        attn = [_attn_pattern(proj, dil, bsz, seq) for _, dil in A_PATTERNS]
        hb, xc, c1, n1, m1, buf = _mlstm_prompt(proj, lw, bsz, seq)
        pv = proj.reshape(bsz, seq, AB_IN)
        new_k = pv[:, seq - keep:, A_WIDTH:2 * A_WIDTH].reshape(bsz, keep, A_HEADS, A_HEAD_DIM)
        new_v = pv[:, seq - keep:, 2 * A_WIDTH:3 * A_WIDTH].reshape(bsz, keep, A_HEADS, A_HEAD_DIM)
        x = _ab_out(x, proj, lw, tm=256, attn=attn, mlstm=(hb, xc))
        for j, o in enumerate((new_k, new_v, c1, n1, m1, buf)):
            outs[j].append(o)
        if li < len(c_layers):
            lc = c_layers[li]
            zeros = jnp.zeros((seq, LANES), F32)
            projc = _norm_inproj(x, lc["norm_pre"], lc["w_in"], zeros, zeros, n_rope=0, tm=tm_in)
            x, h_last, cbuf = _rglru_prompt(x, projc, lc, bsz, seq, 256)
            outs[6].append(h_last)
            outs[7].append(cbuf)
    return x.reshape(bsz, seq, D_MODEL), [jnp.stack(o, axis=0) for o in outs]


def _sample_trunk(x_sample, cache_k, cache_v, st, ab_layers, c_layers):
    bd = x_sample.shape[0]
    x = x_sample.reshape(bd, D_MODEL)
    cos_t, sin_t = _rope_tables(jnp.full((bd,), PAST_LEN, dtype=jnp.int32))
    zeros = jnp.zeros((bd, LANES), F32)
    outs = [[] for _ in range(8)]
    for li in range(len(ab_layers)):
        lw = ab_layers[li]
        proj = _norm_inproj(x, lw["norm_pre"], lw["w_in"], cos_t, sin_t, n_rope=2, tm=bd)
        qkv = proj[:, :3 * A_WIDTH].reshape(bd, 3 * A_HEADS, A_HEAD_DIM)
        oa = _decode_attn(qkv, cache_k, cache_v, li)
        dec, c1, n1, m1, buf = _mlstm_decode(proj, st[3][li], st[0][li], st[1][li], st[2][li], lw)
        dec["oa"] = oa.reshape(bd, A_WIDTH)
        new_k = qkv[:, A_HEADS:2 * A_HEADS].reshape(bd, 1, A_HEADS, A_HEAD_DIM)
        new_v = qkv[:, 2 * A_HEADS:].reshape(bd, 1, A_HEADS, A_HEAD_DIM)
        x = _ab_out(x, proj, lw, tm=bd, decode_in=dec)
        for j, o in enumerate((new_k, new_v, c1, n1, m1, buf)):
            outs[j].append(o)
        if li < len(c_layers):
            lc = c_layers[li]
            projc = _norm_inproj(x, lc["norm_pre"], lc["w_in"], zeros, zeros, n_rope=0, tm=bd)
            x, h, cbuf = _rglru_decode(x, projc, st[5][li], st[4][li], lc)
            outs[6].append(h)
            outs[7].append(cbuf)
    return x.reshape(bd, 1, D_MODEL), [jnp.stack(o, axis=0) for o in outs]


def kernel(x_prompt, x_sample, cache_k, cache_v, state_mlstm_c, state_mlstm_n, state_mlstm_m, state_mlstm_conv,
           state_rglru_h, state_rglru_conv, ab_norm_pre, ab_norm_post, ab_w_in, ab_w_out, mlstm_conv_w,
           mlstm_conv_b, mlstm_wq, mlstm_wk, mlstm_wv, mlstm_w_gates, mlstm_b_gates, mlstm_out_norm, mlstm_skip,
           c_norm_pre, c_norm_post, c_w_in, c_w_out, c_conv_w, c_conv_b, c_w_a, c_b_a, c_w_x, c_b_x, c_lambda):
    n_ab = ab_w_in.shape[0]
    n_c = c_w_in.shape[0]
    ab_layers = [_prep_ab(li, ab_norm_pre, ab_norm_post, ab_w_in, ab_w_out, mlstm_conv_w, mlstm_conv_b, mlstm_wq,
                          mlstm_wk, mlstm_wv, mlstm_w_gates, mlstm_b_gates, mlstm_out_norm, mlstm_skip)
                 for li in range(n_ab)]
    c_layers = [_prep_c(li, c_norm_pre, c_norm_post, c_w_in, c_w_out, c_conv_w, c_conv_b, c_w_a, c_b_a, c_w_x,
                        c_b_x, c_lambda) for li in range(n_c)]
    y_prompt, pst = _prompt_trunk(x_prompt, ab_layers, c_layers)
    st = (state_mlstm_c, state_mlstm_n, state_mlstm_m, state_mlstm_conv, state_rglru_h, state_rglru_conv)
    y_sample, sst = _sample_trunk(x_sample, cache_k, cache_v, st, ab_layers, c_layers)
    return (y_prompt, y_sample, *pst, *sst)
```

```python
import functools

import jax
import jax.numpy as jnp
from jax import lax
from jax.experimental import pallas as pl
from jax.experimental.pallas import tpu as pltpu

F32 = jnp.float32
BF16 = jnp.bfloat16

D_MODEL = 1024
PAST_LEN = 2048
A_HEADS = 8
A_HEAD_DIM = 128
A_WIDTH = 1024
A_PATTERNS = ((128, 1), (512, 4), (2048, 16))
A_MAX_WINDOW = 2048
ROPE_THETA = 10000.0
B_HEADS = 4
B_WIDTH = 1024
B_HEAD_DIM = 256
B_QKV_BLOCK = 4
B_CHUNK = 128
C_WIDTH = 1536
C_BLOCKS = 12
C_BLOCK = 128
RG_LRU_C = 8.0
CONV_WIDTH = 4
AB_IN = 6 * 1024
C_IN = 2 * C_WIDTH
NORM_EPS = 1e-6

LANES = 128
SUBLANES = 8
VMEM_LIMIT = 56 * 1024 * 1024

TM_INPROJ = 256
TM_OUT = 256
TM_RGLRU = 256
ATT_TQ = A_MAX_WINDOW
ATT_BLK = 128
NB_DECODE_ATTN = 4
NB_DECODE_STATE = 8


def _cparams(*sem):
    return pltpu.CompilerParams(dimension_semantics=sem, vmem_limit_bytes=VMEM_LIMIT)


def _sigmoid(x):
    return 1.0 / (1.0 + jnp.exp(-x))


def _silu(x):
    return x * _sigmoid(x)


def _log_sigmoid(x):
    return jnp.minimum(x, 0.0) - jnp.log1p(jnp.exp(-jnp.abs(x)))


def _softplus(x):
    return jnp.maximum(x, 0.0) + jnp.log1p(jnp.exp(-jnp.abs(x)))


def _rmsnorm(x, g):
    return x * lax.rsqrt(jnp.mean(x * x, axis=-1, keepdims=True) + NORM_EPS) * g


def _dot(a, b):
    return jnp.dot(a, b, preferred_element_type=F32)


def _dot_nt(a, b):
    return lax.dot_general(a, b, (((1,), (1,)), ((), ())), preferred_element_type=F32)


def _dot_tn(a, b):
    return lax.dot_general(a, b, (((0,), (0,)), ((), ())), preferred_element_type=F32)


def _inproj_kernel(x_ref, g_ref, w_ref, cos_ref, sin_ref, o_ref, *, n_rope):
    h = _rmsnorm(x_ref[...], g_ref[...]).astype(BF16)
    group = A_WIDTH
    for j in range(w_ref.shape[1] // group):
        p = _dot(h, w_ref[:, j * group:(j + 1) * group])
        if j < n_rope:
            c = cos_ref[...]
            s = sin_ref[...]
            for hd in range(group // A_HEAD_DIM):
                ph = p[:, hd * A_HEAD_DIM:(hd + 1) * A_HEAD_DIM]
                lo = j * group + hd * A_HEAD_DIM
                o_ref[:, lo:lo + A_HEAD_DIM] = ph * c + pltpu.roll(ph, A_HEAD_DIM // 2, 1) * s
        else:
            o_ref[:, j * group:(j + 1) * group] = p


def _norm_inproj(x, g, w_bf16, cos_t, sin_t, *, n_rope, tm):
    t, d = x.shape
    n = w_bf16.shape[1]
    n_pos_tiles = cos_t.shape[0] // tm
    return pl.pallas_call(
        functools.partial(_inproj_kernel, n_rope=n_rope),
        grid=(t // tm,),
        in_specs=[
            pl.BlockSpec((tm, d), lambda i: (i, 0)),
            pl.BlockSpec((1, d), lambda i: (0, 0)),
            pl.BlockSpec((d, n), lambda i: (0, 0)),
            pl.BlockSpec((tm, LANES), lambda i: (i % n_pos_tiles, 0)),
            pl.BlockSpec((tm, LANES), lambda i: (i % n_pos_tiles, 0)),
        ],
        out_specs=pl.BlockSpec((tm, n), lambda i: (i, 0)),
        out_shape=jax.ShapeDtypeStruct((t, n), F32),
        compiler_params=_cparams("parallel"),
        name="norm_inproj",
    )(x, g.reshape(1, d), w_bf16, cos_t, sin_t)


def _attend(q, k, v, bias):
    sc = _dot_nt(q, k) * (A_HEAD_DIM ** -0.5) + bias
    m = jnp.max(sc, axis=-1, keepdims=True)
    p = jnp.exp(sc - m)
    l = jnp.sum(p, axis=-1, keepdims=True)
    o = _dot(p.astype(BF16), v) / l
    return o, jnp.broadcast_to(m + jnp.log(l), (ATT_BLK, LANES))


def _attn_kernel(q_ref, k_ref, v_ref, o_ref, d4f, q1, q4, q16, k1e, k4e, k16e, v1e, v4e, v16e,
                 o1n, o4n, o16n, l1n, l4n, l16n):
    i = pl.program_id(2)
    blk = ATT_BLK
    tq = ATT_TQ
    n4 = tq // 4
    n16 = tq // 16

    @pl.when(i == 0)
    def _():
        for r in (k1e, v1e):
            r[0:blk, :] = jnp.zeros((blk, LANES), BF16)
        for r in (k4e, v4e, k16e, v16e):
            r[:, 0:blk, :] = jnp.zeros((r.shape[0], blk, LANES), BF16)

    @pl.when(i > 0)
    def _():
        for r in (k1e, v1e):
            r[0:blk, :] = r[tq:tq + blk, :]
        for r in (k4e, v4e):
            r[:, 0:blk, :] = r[:, n4:n4 + blk, :]
        for r in (k16e, v16e):
            r[:, 0:blk, :] = r[:, n16:n16 + blk, :]

    def split(src_ref, d1, d4, d16, off):
        def natural(c, carry):
            r0 = pl.multiple_of(c * 2 * blk, 2 * blk)
            d1[pl.ds(off + r0, 2 * blk), :] = src_ref[0, pl.ds(r0, 2 * blk), :].astype(BF16)
            return carry
        lax.fori_loop(0, tq // (2 * blk), natural, 0)
        for r4 in range(4):
            for c in range(n4 // blk):
                x = src_ref[0, pl.ds(r4 + 4 * c * blk, blk, stride=4), :]
                d4f[r4 * n4 + c * blk:r4 * n4 + (c + 1) * blk, :] = x
                d4[r4, off + c * blk:off + (c + 1) * blk, :] = x.astype(BF16)
        for r4 in range(4):
            for a in range(4):
                x = d4f[pl.ds(r4 * n4 + a, n16, stride=4), :]
                d16[r4 + 4 * a, off:off + n16, :] = x.astype(BF16)

    split(q_ref, q1, q4, q16, 0)
    split(k_ref, k1e, k4e, k16e, blk)
    split(v_ref, v1e, v4e, v16e, blk)

    row = lax.broadcasted_iota(jnp.int32, (blk, 2 * blk), 0)
    col = lax.broadcasted_iota(jnp.int32, (blk, 2 * blk), 1)
    bias = jnp.where((col >= row) & (col - row <= blk), 0.0, -jnp.inf)
    no_prev = jnp.where(i == 0, blk, 0)
    bias_first = jnp.where(col >= no_prev, bias, -jnp.inf)

    for s in range(tq // blk):
        rows = slice(s * blk, (s + 1) * blk)
        win = slice(s * blk, (s + 2) * blk)
        o, lse = _attend(q1[rows, :], k1e[win, :], v1e[win, :], bias_first if s == 0 else bias)
        o1n[rows, :] = o
        l1n[rows, :] = lse

    for r4 in range(4):
        for s in range(n4 // blk):
            win = slice(s * blk, (s + 2) * blk)
            o, lse = _attend(q4[r4, s * blk:(s + 1) * blk, :], k4e[r4, win, :], v4e[r4, win, :],
                             bias_first if s == 0 else bias)
            o4n[pl.ds(4 * s * blk + r4, blk, stride=4), :] = o
            l4n[pl.ds(4 * s * blk + r4, blk, stride=4), :] = lse

    for r in range(16):
        o, lse = _attend(q16[r], k16e[r], v16e[r], bias_first)
        o16n[pl.ds(r, blk, stride=16), :] = o
        l16n[pl.ds(r, blk, stride=16), :] = lse

    def merge(c, carry):
        r0 = pl.multiple_of(c * blk, blk)
        rows = pl.ds(r0, blk)
        l1, l4, l16 = l1n[rows, :], l4n[rows, :], l16n[rows, :]
        mx = jnp.maximum(jnp.maximum(l1, l4), l16)
        e1, e4, e16 = jnp.exp(l1 - mx), jnp.exp(l4 - mx), jnp.exp(l16 - mx)
        o_ref[0, rows, :] = (o1n[rows, :] * e1 + o4n[rows, :] * e4 + o16n[rows, :] * e16) / (e1 + e4 + e16)
        return carry
    lax.fori_loop(0, tq // blk, merge, 0)


def _dilated_attn(proj, bsz, seq):
    tq, blk = ATT_TQ, ATT_BLK
    view = proj.reshape(bsz, seq, AB_IN)
    col = lambda c: pl.BlockSpec((1, tq, A_HEAD_DIM), lambda b, h, i: (b, i, c * A_HEADS + h))
    bf = lambda *s: pltpu.VMEM(s, BF16)
    nat = pltpu.VMEM((tq, LANES), F32)
    out = pl.pallas_call(
        _attn_kernel,
        grid=(bsz, A_HEADS, seq // tq),
        in_specs=[col(0), col(1), col(2)],
        out_specs=pl.BlockSpec((1, tq, A_HEAD_DIM), lambda b, h, i: (b, i, h)),
        out_shape=jax.ShapeDtypeStruct((bsz, seq, A_WIDTH), F32),
        scratch_shapes=[nat,
                        bf(tq, LANES), bf(4, tq // 4, LANES), bf(16, tq // 16, LANES),
                        bf(tq + blk, LANES), bf(4, tq // 4 + blk, LANES), bf(16, tq // 16 + blk, LANES),
                        bf(tq + blk, LANES), bf(4, tq // 4 + blk, LANES), bf(16, tq // 16 + blk, LANES),
                        nat, nat, nat, nat, nat, nat],
        compiler_params=_cparams("parallel", "parallel", "arbitrary"),
        name="dilated_attn",
    )(view, view, view)
    return out.reshape(bsz * seq, A_WIDTH)


def _decode_attn_kernel(q_ref, kn_ref, vn_ref, k1_ref, k4_ref, k16_ref, v1_ref, v4_ref, v16_ref, o_ref, *, nb):
    scale = A_HEAD_DIM ** -0.5
    for bb in range(nb):
        q = q_ref[bb]
        kn = kn_ref[bb]
        vn = vn_ref[bb]
        s_new = jnp.sum(q * kn, axis=-1, keepdims=True) * scale
        outs, lses = [], []
        for k_ref, v_ref in ((k1_ref, v1_ref), (k4_ref, v4_ref), (k16_ref, v16_ref)):
            kk = k_ref[0, bb, :, 0]
            vv = v_ref[0, bb, :, 0]
            sc = jnp.sum(kk * q[None], axis=-1, keepdims=True) * scale
            m = jnp.maximum(jnp.max(sc, axis=0), s_new)
            p = jnp.exp(sc - m[None])
            p_new = jnp.exp(s_new - m)
            l = jnp.sum(p, axis=0) + p_new
            outs.append((jnp.sum(p * vv, axis=0) + p_new * vn) / l)
            lses.append(m + jnp.log(l))
        mx = jnp.maximum(jnp.maximum(lses[0], lses[1]), lses[2])
        e = [jnp.exp(x - mx) for x in lses]
        tot = e[0] + e[1] + e[2]
        o_ref[bb] = (outs[0] * e[0] + outs[1] * e[1] + outs[2] * e[2]) / tot


def _decode_attn(qkv, cache_k, cache_v, li):
    bd = qkv.shape[0]
    n_layers, _, win = cache_k.shape[:3]
    nb = NB_DECODE_ATTN
    new = lambda c: pl.BlockSpec((nb, A_HEADS, A_HEAD_DIM), lambda i: (i, c, 0))
    specs, views = [], []
    for cache in (cache_k, cache_v):
        for window, dil in A_PATTERNS:
            rows = win // dil
            n_keys = window // dil
            views.append(cache.reshape(n_layers, bd, rows, dil, A_HEADS, A_HEAD_DIM))
            specs.append(pl.BlockSpec((1, nb, n_keys, 1, A_HEADS, A_HEAD_DIM),
                                      lambda i, rb=rows // n_keys - 1: (li, i, rb, 0, 0, 0)))
    return pl.pallas_call(
        functools.partial(_decode_attn_kernel, nb=nb),
        grid=(bd // nb,),
        in_specs=[new(0), new(1), new(2)] + specs,
        out_specs=pl.BlockSpec((nb, A_HEADS, A_HEAD_DIM), lambda i: (i, 0, 0)),
        out_shape=jax.ShapeDtypeStruct((bd, A_HEADS, A_HEAD_DIM), F32),
        compiler_params=_cparams("parallel"),
        name="decode_attn",
    )(qkv, qkv, qkv, *views)


_SHIFTS = tuple(range(-(B_QKV_BLOCK - 1), B_QKV_BLOCK))


def _blockdiag4_weights(w):
    nb = w.shape[0]
    rows = []
    for s in _SHIFTS:
        cols = []
        for o in range(B_QKV_BLOCK):
            i = o + s
            cols.append(w[:, i, o] if 0 <= i < B_QKV_BLOCK else jnp.zeros((nb,), w.dtype))
        rows.append(jnp.stack(cols, axis=1).reshape(nb * B_QKV_BLOCK))
    return jnp.stack(rows, axis=0)


def _lane_shifts(x):
    w = x.shape[1]
    out = []
    for s in _SHIFTS:
        if s == 0:
            out.append(x)
        else:
            k = (-s) % LANES
            out.append(jnp.concatenate(
                [pltpu.roll(x[:, g * LANES:(g + 1) * LANES], k, 1) for g in range(w // LANES)], axis=1))
    return out


def _blockdiag4(shifted, w7_ref):
    acc = shifted[0] * w7_ref[0:1, :]
    for i in range(1, len(_SHIFTS)):
        acc = acc + shifted[i] * w7_ref[i:i + 1, :]
    return acc


def _mlstm_qkv(xb, xc, wq7_ref, wk7_ref, wv7_ref):
    sh_c = _lane_shifts(xc)
    qb = _blockdiag4(sh_c, wq7_ref)
    kb = _blockdiag4(sh_c, wk7_ref)
    vb = _blockdiag4(_lane_shifts(xb), wv7_ref)
    return qb, kb, vb


def _head_out(hb_raw, xc, on_ref, sk_ref):
    parts = []
    for h in range(B_HEADS):
        xh = hb_raw[:, h * B_HEAD_DIM:(h + 1) * B_HEAD_DIM]
        xcn = xh - jnp.mean(xh, axis=-1, keepdims=True)
        parts.append(xcn * lax.rsqrt(jnp.mean(xcn * xcn, axis=-1, keepdims=True) + 1e-5))
    return jnp.concatenate(parts, axis=1) * on_ref[...] + sk_ref[...] * xc


def _cumsum_lanes(x):
    t = x.shape[1]
    idx = lax.broadcasted_iota(jnp.int32, x.shape, 1)
    k = 1
    while k < t:
        x = x + jnp.where(idx >= k, pltpu.roll(x, k, 1), 0.0)
        k *= 2
    return x


def _mlstm_prompt_kernel(xb_ref, cw_ref, cb_ref, wq7_ref, wk7_ref, wv7_ref, wgt_ref, bgt_ref, on_ref, sk_ref,
                         hb_ref, c_out, n_out, m_out, buf_out,
                         xx_scr, hraw_scr, c_scr, n_scr, m_scr):
    ci = pl.program_id(1)
    nc = pl.num_programs(1)
    t = B_CHUNK
    halo = SUBLANES
    ng = 2 * B_HEADS

    @pl.when(ci == 0)
    def _():
        xx_scr[0:halo, :] = jnp.zeros((halo, B_WIDTH), F32)
        c_scr[...] = jnp.zeros_like(c_scr)
        n_scr[...] = jnp.zeros_like(n_scr)
        m_scr[...] = jnp.zeros_like(m_scr)

    x = xb_ref[0]
    xx_scr[halo:halo + t, :] = x
    xconv = x * cw_ref[3:4, :] + cb_ref[...]
    for s in range(1, CONV_WIDTH):
        xconv = xconv + xx_scr[pl.ds(halo - s, t), :] * cw_ref[3 - s:4 - s, :]
    xx_scr[0:halo, :] = x[t - halo:t, :]
    xc = _silu(xconv)

    qb, kb, vb = _mlstm_qkv(x, xc, wq7_ref, wk7_ref, wv7_ref)
    qh, kh, vh = qb.astype(BF16), kb.astype(BF16), vb.astype(BF16)
    gt = (_dot_nt(wgt_ref[:, 0:B_WIDTH], qh) + _dot_nt(wgt_ref[:, B_WIDTH:2 * B_WIDTH], kh)
          + _dot_nt(wgt_ref[:, 2 * B_WIDTH:3 * B_WIDTH], vh) + bgt_ref[...])
    brow = _cumsum_lanes(_log_sigmoid(gt))
    gcol = jnp.concatenate([gt, brow, jnp.zeros((t - 2 * ng, t), F32)], axis=0).T
    ri = lax.broadcasted_iota(jnp.int32, (t, t), 0)
    cj = lax.broadcasted_iota(jnp.int32, (t, t), 1)
    causal = cj <= ri
    kscale = B_HEAD_DIM ** -0.5
    for h in range(B_HEADS):
        sl = slice(h * B_HEAD_DIM, (h + 1) * B_HEAD_DIM)
        q = qb[:, sl]
        k = kb[:, sl] * kscale
        qhh = qh[:, sl]
        khh = k.astype(BF16)
        vhh = vh[:, sl]
        logi_c = gcol[:, h:h + 1]
        b_c = gcol[:, ng + B_HEADS + h:ng + B_HEADS + h + 1]
        logi_r = gt[h:h + 1, :]
        b_r = brow[B_HEADS + h:B_HEADS + h + 1, :]
        m0 = m_scr[h:h + 1, 0:1]
        n0 = n_scr[h:h + 1, :]
        c0 = c_scr[h]
        dmat = jnp.where(causal, b_c - b_r + logi_r, -jnp.inf)
        inter = b_c + m0
        m = jnp.maximum(jnp.max(dmat, axis=1, keepdims=True), inter)
        w_intra = jnp.exp(dmat - m)
        w_inter = jnp.exp(inter - m)
        s = _dot_nt(qhh, khh) * w_intra
        num = _dot(s.astype(BF16), vhh) + w_inter * _dot(qhh, c0.astype(BF16))
        den = jnp.sum(s, axis=1, keepdims=True) + w_inter * jnp.sum(q * n0, axis=1, keepdims=True)
        hraw_scr[:, sl] = num / jnp.maximum(jnp.abs(den), jnp.exp(-m))
        bl = b_c[t - 1:t, :]
        dl = bl - b_c + logi_c
        m_new = jnp.maximum(bl + m0, jnp.max(dl, axis=0, keepdims=True))
        kw = jnp.exp(dl - m_new) * k
        carry = jnp.exp(bl + m0 - m_new)
        c_scr[h] = carry * c0 + _dot_tn(kw.astype(BF16), vhh)
        n_scr[h:h + 1, :] = carry * n0 + jnp.sum(kw, axis=0, keepdims=True)
        m_scr[h:h + 1, :] = jnp.broadcast_to(m_new, (1, LANES))
    hb_ref[0] = _head_out(hraw_scr[...], xc, on_ref, sk_ref)

    @pl.when(ci == nc - 1)
    def _():
        c_out[0] = c_scr[...]
        n_out[0] = n_scr[0:B_HEADS, :]
        lane = lax.broadcasted_iota(jnp.int32, (1, LANES), 1)
        mt = jnp.zeros((1, LANES), F32)
        for h in range(B_HEADS):
            mt = jnp.where(lane == h, m_scr[h:h + 1, :], mt)
        m_out[0] = mt
        buf_out[0] = x[t - halo:t, :]


def _mlstm_prompt(proj, lw, bsz, seq):
    t = B_CHUNK
    view = proj.reshape(bsz, seq, AB_IN)
    full = lambda a: pl.BlockSpec(a.shape, lambda b, c: (0,) * a.ndim)
    params = [lw["conv_w"], lw["conv_b"], lw["wq7"], lw["wk7"], lw["wv7"], lw["wgt"], lw["bgt"],
              lw["out_norm"], lw["skip"]]
    sds = jax.ShapeDtypeStruct
    hb, c, n, m, buf = pl.pallas_call(
        _mlstm_prompt_kernel,
        grid=(bsz, seq // t),
        in_specs=[pl.BlockSpec((1, t, B_WIDTH), lambda b, c: (b, c, 4))] + [full(a) for a in params],
        out_specs=[pl.BlockSpec((1, t, B_WIDTH), lambda b, c: (b, c, 0)),
                   pl.BlockSpec((1, B_HEADS, B_HEAD_DIM, B_HEAD_DIM), lambda b, c: (b, 0, 0, 0)),
                   pl.BlockSpec((1, B_HEADS, B_HEAD_DIM), lambda b, c: (b, 0, 0)),
                   pl.BlockSpec((1, 1, LANES), lambda b, c: (b, 0, 0)),
                   pl.BlockSpec((1, SUBLANES, B_WIDTH), lambda b, c: (b, 0, 0))],
        out_shape=[sds((bsz, seq, B_WIDTH), F32),
                   sds((bsz, B_HEADS, B_HEAD_DIM, B_HEAD_DIM), F32),
                   sds((bsz, B_HEADS, B_HEAD_DIM), F32),
                   sds((bsz, 1, LANES), F32),
                   sds((bsz, SUBLANES, B_WIDTH), F32)],
        scratch_shapes=[pltpu.VMEM((t + SUBLANES, B_WIDTH), F32),
                        pltpu.VMEM((t, B_WIDTH), F32),
                        pltpu.VMEM((B_HEADS, B_HEAD_DIM, B_HEAD_DIM), F32),
                        pltpu.VMEM((SUBLANES, B_HEAD_DIM), F32),
                        pltpu.VMEM((SUBLANES, LANES), F32)],
        compiler_params=_cparams("parallel", "arbitrary"),
        name="mlstm_prompt",
    )(view, *params)
    return (hb.reshape(bsz * seq, B_WIDTH), c, n, m[:, 0, :B_HEADS], buf[:, SUBLANES - (CONV_WIDTH - 1):, :])


def _mlstm_decode_pre_kernel(xb_ref, b0_ref, b1_ref, b2_ref, cw_ref, cb_ref, wq7_ref, wk7_ref, wv7_ref, wg_ref,
                             bg_ref, n0_ref, m0_ref,
                             xc_ref, v_ref, sv_ref, wi_ref, den_ref, n_out, m_out, buf_out, qt_ref, kwt_ref):
    x = xb_ref[...]
    xconv = (x * cw_ref[3:4, :] + b2_ref[...] * cw_ref[2:3, :] + b1_ref[...] * cw_ref[1:2, :]
             + b0_ref[...] * cw_ref[0:1, :] + cb_ref[...])
    xc = _silu(xconv)
    xc_ref[...] = xc
    buf_out[:, 0:B_WIDTH] = b1_ref[...]
    buf_out[:, B_WIDTH:2 * B_WIDTH] = b2_ref[...]
    buf_out[:, 2 * B_WIDTH:3 * B_WIDTH] = x
    qb, kb, vb = _mlstm_qkv(x, xc, wq7_ref, wk7_ref, wv7_ref)
    g = (_dot(qb.astype(BF16), wg_ref[0:B_WIDTH, :]) + _dot(kb.astype(BF16), wg_ref[B_WIDTH:2 * B_WIDTH, :])
         + _dot(vb.astype(BF16), wg_ref[2 * B_WIDTH:3 * B_WIDTH, :]) + bg_ref[...])
    v_ref[...] = vb
    logf = _log_sigmoid(g)
    rows = x.shape[0]
    lane = lax.broadcasted_iota(jnp.int32, (rows, LANES), 1)
    m_tile = jnp.zeros((rows, LANES), F32)
    kscale = B_HEAD_DIM ** -0.5
    for h in range(B_HEADS):
        sl = slice(h * B_HEAD_DIM, (h + 1) * B_HEAD_DIM)
        logi = g[:, h:h + 1]
        fm = logf[:, B_HEADS + h:B_HEADS + h + 1] + m0_ref[:, h:h + 1]
        m = jnp.maximum(logi, fm)
        w_intra = jnp.exp(logi - m)
        w_inter = jnp.exp(fm - m)
        q = qb[:, sl]
        k = kb[:, sl] * kscale
        n0 = n0_ref[:, sl]
        s = jnp.sum(q * k, axis=1, keepdims=True) * w_intra
        den = s + w_inter * jnp.sum(q * n0, axis=1, keepdims=True)
        kw = w_intra * k
        sv_ref[:, sl] = s * vb[:, sl]
        wi_ref[:, sl] = jnp.broadcast_to(w_inter, q.shape)
        den_ref[:, sl] = jnp.broadcast_to(jnp.maximum(jnp.abs(den), jnp.exp(-m)), q.shape)
        n_out[:, sl] = w_inter * n0 + kw
        m_tile = jnp.where(lane == h, m, m_tile)
        qt_ref[h] = q.T
        kwt_ref[h] = kw.T
    m_out[...] = m_tile


def _column(mat_t, b):
    lane = lax.broadcasted_iota(jnp.int32, mat_t.shape, 1)
    return jnp.sum(jnp.where(lane == b, mat_t, 0.0), axis=1, keepdims=True)


def _mlstm_decode_qc_kernel(c0_ref, qt_ref, qc_ref, *, nb):
    i = pl.program_id(1)
    qt = qt_ref[0]
    for bb in range(nb):
        qc_ref[bb] = jnp.sum(c0_ref[0, bb, 0] * _column(qt, i * nb + bb), axis=0, keepdims=True)


def _mlstm_decode_cnew_kernel(c0_ref, kwt_ref, v_ref, wi_ref, c_out, *, nb):
    i = pl.program_id(2)
    kwt = kwt_ref[0, 0]
    for bb in range(nb):
        c_out[0, bb, 0] = wi_ref[0, bb] * c0_ref[0, bb, 0] + _column(kwt, i * nb + bb) * v_ref[0, bb]


def _mlstm_decode(proj, conv_state, state_c, li, n0, m0, lw):
    bd = proj.shape[0]
    w = B_WIDTH
    cs = conv_state.reshape(bd, (CONV_WIDTH - 1) * w)
    m0p = jnp.pad(m0, ((0, 0), (0, LANES - B_HEADS)))
    row = lambda c: pl.BlockSpec((bd, w), lambda i: (0, c))
    full = lambda a: pl.BlockSpec(a.shape, lambda i: (0,) * a.ndim)
    params = [lw["conv_w"], lw["conv_b"], lw["wq7"], lw["wk7"], lw["wv7"], lw["wg"], lw["bg"]]
    n0f = n0.reshape(bd, w)
    sds = jax.ShapeDtypeStruct
    tr = pl.BlockSpec((B_HEADS, B_HEAD_DIM, bd), lambda i: (0, 0, 0))
    outs = pl.pallas_call(
        _mlstm_decode_pre_kernel,
        grid=(1,),
        in_specs=[row(4), row(0), row(1), row(2)] + [full(a) for a in params] + [full(n0f), full(m0p)],
        out_specs=[row(0)] * 6 + [pl.BlockSpec((bd, LANES), lambda i: (0, 0)),
                                  pl.BlockSpec((bd, 3 * w), lambda i: (0, 0)), tr, tr],
        out_shape=[sds((bd, w), F32)] * 6 + [sds((bd, LANES), F32), sds((bd, 3 * w), F32),
                                             sds((B_HEADS, B_HEAD_DIM, bd), F32),
                                             sds((B_HEADS, B_HEAD_DIM, bd), F32)],
        compiler_params=_cparams("arbitrary"),
        name="mlstm_decode_pre",
    )(proj, cs, cs, cs, *params, n0f, m0p)
    xc, v, sv, wi, den, n_new, m_new, buf, qt, kwt = outs
    nb = NB_DECODE_STATE
    qc = pl.pallas_call(
        functools.partial(_mlstm_decode_qc_kernel, nb=nb),
        grid=(B_HEADS, bd // nb),
        in_specs=[pl.BlockSpec((1, nb, 1, B_HEAD_DIM, B_HEAD_DIM), lambda h, i: (li, i, h, 0, 0)),
                  pl.BlockSpec((1, B_HEAD_DIM, bd), lambda h, i: (h, 0, 0))],
        out_specs=pl.BlockSpec((nb, 1, B_HEAD_DIM), lambda h, i: (i, 0, h)),
        out_shape=sds((bd, 1, w), F32),
        compiler_params=_cparams("parallel", "arbitrary"),
        name="mlstm_decode_qc",
    )(state_c, qt)
    dec = dict(xc=xc, sv=sv, wi=wi, den=den, qc=qc.reshape(bd, w))
    upd = dict(kwt=kwt, v=v.reshape(bd, 1, w), wi=wi.reshape(bd, 1, w))
    return (dec, upd, n_new.reshape(bd, B_HEADS, B_HEAD_DIM), m_new[:, :B_HEADS],
            buf.reshape(bd, CONV_WIDTH - 1, w))


def _mlstm_decode_cnew(state_c, upds):
    n_layers, bd = state_c.shape[:2]
    nb = NB_DECODE_STATE
    kwt = jnp.stack([u["kwt"] for u in upds], axis=0)
    v = jnp.stack([u["v"] for u in upds], axis=0)
    wi = jnp.stack([u["wi"] for u in upds], axis=0)
    cblk = pl.BlockSpec((1, nb, 1, B_HEAD_DIM, B_HEAD_DIM), lambda l, h, i: (l, i, h, 0, 0))
    rowh = pl.BlockSpec((1, nb, 1, B_HEAD_DIM), lambda l, h, i: (l, i, 0, h))
    return pl.pallas_call(
        functools.partial(_mlstm_decode_cnew_kernel, nb=nb),
        grid=(n_layers, B_HEADS, bd // nb),
        in_specs=[cblk, pl.BlockSpec((1, 1, B_HEAD_DIM, bd), lambda l, h, i: (l, h, 0, 0)), rowh, rowh],
        out_specs=cblk,
        out_shape=jax.ShapeDtypeStruct(state_c.shape, F32),
        compiler_params=_cparams("parallel", "parallel", "arbitrary"),
        name="mlstm_decode_cnew",
    )(state_c, kwt, v, wi)


def _ab_out_kernel(*refs, decode):
    if decode:
        (oa_ref, sv_ref, wi_ref, qc_ref, den_ref, xc_ref, za_ref, zb_ref, x_ref,
         w_ref, on_ref, sk_ref, g_ref, o_ref) = refs
        hb = _head_out((sv_ref[...] + wi_ref[...] * qc_ref[...]) / den_ref[...], xc_ref[...], on_ref, sk_ref)
    else:
        oa_ref, hb_ref, za_ref, zb_ref, x_ref, w_ref, on_ref, sk_ref, g_ref, o_ref = refs
        hb = hb_ref[...]
    ya = oa_ref[...] * _silu(za_ref[...])
    yb = hb * _silu(zb_ref[...])
    y = _dot(ya.astype(BF16), w_ref[0:A_WIDTH, :]) + _dot(yb.astype(BF16), w_ref[A_WIDTH:A_WIDTH + B_WIDTH, :])
    o_ref[...] = x_ref[...] + _rmsnorm(y, g_ref[...])


def _ab_out(x, proj, lw, *, tm, oa, hb=None, dec=None):
    t = x.shape[0]
    w = D_MODEL
    row = pl.BlockSpec((tm, w), lambda i: (i, 0))
    pcol = lambda c: pl.BlockSpec((tm, w), lambda i: (i, c))
    full = lambda a: pl.BlockSpec(a.shape, lambda i: (0,) * a.ndim)
    params = [lw["w_out"], lw["out_norm"], lw["skip"], lw["norm_post"]]
    if dec is not None:
        ins = [oa, dec["sv"], dec["wi"], dec["qc"], dec["den"], dec["xc"], proj, proj, x]
        specs = [row] * 6 + [pcol(3), pcol(5), row]
    else:
        ins = [oa, hb, proj, proj, x]
        specs = [row, row, pcol(3), pcol(5), row]
    return pl.pallas_call(
        functools.partial(_ab_out_kernel, decode=dec is not None),
        grid=(t // tm,),
        in_specs=specs + [full(a) for a in params],
        out_specs=row,
        out_shape=jax.ShapeDtypeStruct((t, w), F32),
        compiler_params=_cparams("parallel"),
        name="ab_out",
    )(*ins, *params)


def _rglru_gates(xconv, wa_ref, ba_ref, wx_ref, bx_ref, lam_ref):
    rs, is_ = [], []
    for n in range(C_BLOCKS):
        sl = slice(n * C_BLOCK, (n + 1) * C_BLOCK)
        xa = xconv[:, sl].astype(BF16)
        rs.append(_dot(xa, wa_ref[n]))
        is_.append(_dot(xa, wx_ref[n]))
    r = _sigmoid(jnp.concatenate(rs, axis=1) + ba_ref[...])
    ig = _sigmoid(jnp.concatenate(is_, axis=1) + bx_ref[...])
    log_a = -RG_LRU_C * r * _softplus(-lam_ref[...])
    a = jnp.exp(log_a)
    u = jnp.sqrt(1.0 - a * a) * (ig * xconv)
    return a, u


def _rglru_out(h, z, x, w_ref, g_ref):
    y = _dot((h * _silu(z)).astype(BF16), w_ref[...])
    return x + _rmsnorm(y, g_ref[...])


def _rglru_prompt_kernel(xc_ref, zc_ref, x_ref, cw_ref, cb_ref, wa_ref, ba_ref, wx_ref, bx_ref, lam_ref,
                         w_ref, g_ref, o_ref, h_out, buf_out, xx_scr, a_scr, u_scr, hc_scr, *, tm):
    ti = pl.program_id(1)
    nt = pl.num_programs(1)
    halo = SUBLANES

    @pl.when(ti == 0)
    def _():
        xx_scr[0:halo, :] = jnp.zeros((halo, C_WIDTH), F32)
        hc_scr[...] = jnp.zeros_like(hc_scr)

    x = xc_ref[0]
    xx_scr[halo:halo + tm, :] = x
    xconv = x * cw_ref[3:4, :] + cb_ref[...]
    for s in range(1, CONV_WIDTH):
        xconv = xconv + xx_scr[pl.ds(halo - s, tm), :] * cw_ref[3 - s:4 - s, :]
    xx_scr[0:halo, :] = x[tm - halo:tm, :]
    a, u = _rglru_gates(xconv, wa_ref, ba_ref, wx_ref, bx_ref, lam_ref)
    a_scr[...] = a
    u_scr[...] = u
    ridx = lax.broadcasted_iota(jnp.int32, (SUBLANES, C_WIDTH), 0)

    def body(r, hprev):
        r8 = pl.multiple_of(r * SUBLANES, SUBLANES)
        aa = a_scr[pl.ds(r8, SUBLANES), :]
        uu = u_scr[pl.ds(r8, SUBLANES), :]
        for sft in (1, 2, 4):
            keep = ridx >= sft
            a_sh = jnp.where(keep, pltpu.roll(aa, sft, 0), 1.0)
            u_sh = jnp.where(keep, pltpu.roll(uu, sft, 0), 0.0)
            uu = aa * u_sh + uu
            aa = aa * a_sh
        hblk = aa * hprev + uu
        u_scr[pl.ds(r8, SUBLANES), :] = hblk
        return hblk[SUBLANES - 1:SUBLANES, :]

    hlast = lax.fori_loop(0, tm // SUBLANES, body, hc_scr[0:1, :])
    hc_scr[0:1, :] = hlast
    o_ref[0] = _rglru_out(u_scr[...], zc_ref[0], x_ref[0], w_ref, g_ref)

    @pl.when(ti == nt - 1)
    def _():
        h_out[0] = hlast
        buf_out[0] = x[tm - halo:tm, :]


def _rglru_decode_kernel(xc_ref, zc_ref, x_ref, b0_ref, b1_ref, b2_ref, h0_ref, cw_ref, cb_ref, wa_ref, ba_ref,
                         wx_ref, bx_ref, lam_ref, w_ref, g_ref, o_ref, h_out, buf_out):
    x = xc_ref[...]
    xconv = (x * cw_ref[3:4, :] + b2_ref[...] * cw_ref[2:3, :] + b1_ref[...] * cw_ref[1:2, :]
             + b0_ref[...] * cw_ref[0:1, :] + cb_ref[...])
    a, u = _rglru_gates(xconv, wa_ref, ba_ref, wx_ref, bx_ref, lam_ref)
    h = a * h0_ref[...] + u
    h_out[...] = h
    buf_out[:, 0:C_WIDTH] = b1_ref[...]
    buf_out[:, C_WIDTH:2 * C_WIDTH] = b2_ref[...]
    buf_out[:, 2 * C_WIDTH:3 * C_WIDTH] = x
    o_ref[...] = _rglru_out(h, zc_ref[...], x_ref[...], w_ref, g_ref)


def _c_params(lw):
    return [lw["conv_w"], lw["conv_b"], lw["w_a"], lw["b_a"], lw["w_x"], lw["b_x"], lw["lam"],
            lw["w_out"], lw["norm_post"]]


def _rglru_prompt(x, proj, lw, bsz, seq, tm):
    cw = C_WIDTH
    pv = proj.reshape(bsz, seq, C_IN)
    xv = x.reshape(bsz, seq, D_MODEL)
    params = _c_params(lw)
    full = lambda a: pl.BlockSpec(a.shape, lambda b, i: (0,) * a.ndim)
    sds = jax.ShapeDtypeStruct
    y, h_last, buf = pl.pallas_call(
        functools.partial(_rglru_prompt_kernel, tm=tm),
        grid=(bsz, seq // tm),
        in_specs=[pl.BlockSpec((1, tm, cw), lambda b, i: (b, i, 0)),
                  pl.BlockSpec((1, tm, cw), lambda b, i: (b, i, 1)),
                  pl.BlockSpec((1, tm, D_MODEL), lambda b, i: (b, i, 0))] + [full(a) for a in params],
        out_specs=[pl.BlockSpec((1, tm, D_MODEL), lambda b, i: (b, i, 0)),
                   pl.BlockSpec((1, 1, cw), lambda b, i: (b, 0, 0)),
                   pl.BlockSpec((1, SUBLANES, cw), lambda b, i: (b, 0, 0))],
        out_shape=[sds((bsz, seq, D_MODEL), F32), sds((bsz, 1, cw), F32), sds((bsz, SUBLANES, cw), F32)],
        scratch_shapes=[pltpu.VMEM((tm + SUBLANES, cw), F32), pltpu.VMEM((tm, cw), F32),
                        pltpu.VMEM((tm, cw), F32), pltpu.VMEM((SUBLANES, cw), F32)],
        compiler_params=_cparams("parallel", "arbitrary"),
        name="rglru_prompt",
    )(pv, pv, xv, *params)
    return y.reshape(bsz * seq, D_MODEL), h_last[:, 0, :], buf[:, SUBLANES - (CONV_WIDTH - 1):, :]


def _rglru_decode(x, proj, conv_state, h0, lw):
    bd = x.shape[0]
    cw = C_WIDTH
    cs = conv_state.reshape(bd, (CONV_WIDTH - 1) * cw)
    params = _c_params(lw)
    col = lambda c: pl.BlockSpec((bd, cw), lambda i: (0, c))
    full = lambda a: pl.BlockSpec(a.shape, lambda i: (0,) * a.ndim)
    sds = jax.ShapeDtypeStruct
    y, h, buf = pl.pallas_call(
        _rglru_decode_kernel,
        grid=(1,),
        in_specs=[col(0), col(1), full(x), col(0), col(1), col(2), full(h0)] + [full(a) for a in params],
        out_specs=[pl.BlockSpec((bd, D_MODEL), lambda i: (0, 0)), pl.BlockSpec((bd, cw), lambda i: (0, 0)),
                   pl.BlockSpec((bd, 3 * cw), lambda i: (0, 0))],
        out_shape=[sds((bd, D_MODEL), F32), sds((bd, cw), F32), sds((bd, 3 * cw), F32)],
        compiler_params=_cparams("arbitrary"),
        name="rglru_decode",
    )(proj, proj, x, cs, cs, cs, h0, *params)
    return y, h, buf.reshape(bd, CONV_WIDTH - 1, cw)


def _rope_tables(pos):
    half = A_HEAD_DIM // 2
    freqs = ROPE_THETA ** (-jnp.arange(half, dtype=F32) / half)
    ang = pos.astype(F32)[:, None] * freqs[None, :]
    cos, sin = jnp.cos(ang), jnp.sin(ang)
    return jnp.concatenate([cos, cos], axis=1), jnp.concatenate([-sin, sin], axis=1)


def _prep_ab(li, ab_norm_pre, ab_norm_post, ab_w_in, ab_w_out, mlstm_conv_w, mlstm_conv_b, mlstm_wq, mlstm_wk,
             mlstm_wv, mlstm_w_gates, mlstm_b_gates, mlstm_out_norm, mlstm_skip):
    ng = 2 * B_HEADS
    wg = jnp.pad(mlstm_w_gates[li], ((0, 0), (0, LANES - ng))).astype(BF16)
    return dict(
        norm_pre=ab_norm_pre[li], norm_post=ab_norm_post[li].reshape(1, D_MODEL),
        w_in=ab_w_in[li].astype(BF16), w_out=ab_w_out[li].astype(BF16),
        conv_w=mlstm_conv_w[li], conv_b=mlstm_conv_b[li].reshape(1, B_WIDTH),
        wq7=_blockdiag4_weights(mlstm_wq[li]), wk7=_blockdiag4_weights(mlstm_wk[li]),
        wv7=_blockdiag4_weights(mlstm_wv[li]),
        wg=wg, wgt=mlstm_w_gates[li].T.astype(BF16),
        bg=jnp.pad(mlstm_b_gates[li], (0, LANES - ng)).reshape(1, LANES),
        bgt=mlstm_b_gates[li].reshape(ng, 1),
        out_norm=mlstm_out_norm[li].reshape(1, B_WIDTH), skip=mlstm_skip[li].reshape(1, B_WIDTH))


def _prep_c(li, c_norm_pre, c_norm_post, c_w_in, c_w_out, c_conv_w, c_conv_b, c_w_a, c_b_a, c_w_x, c_b_x,
            c_lambda):
    return dict(
        norm_pre=c_norm_pre[li], norm_post=c_norm_post[li].reshape(1, D_MODEL),
        w_in=c_w_in[li].astype(BF16), w_out=c_w_out[li].astype(BF16),
        conv_w=c_conv_w[li], conv_b=c_conv_b[li].reshape(1, C_WIDTH),
        w_a=c_w_a[li].astype(BF16), b_a=c_b_a[li].reshape(1, C_WIDTH),
        w_x=c_w_x[li].astype(BF16), b_x=c_b_x[li].reshape(1, C_WIDTH),
        lam=c_lambda[li].reshape(1, C_WIDTH))


def _prompt_trunk(x_prompt, ab_layers, c_layers):
    bsz, seq, _ = x_prompt.shape
    x = x_prompt.reshape(bsz * seq, D_MODEL)
    cos_t, sin_t = _rope_tables(jnp.arange(seq, dtype=jnp.int32))
    zeros = jnp.zeros((seq, LANES), F32)
    outs = [[] for _ in range(8)]
    keep = min(A_MAX_WINDOW, seq)
    for li in range(len(ab_layers)):
        lw = ab_layers[li]
        proj = _norm_inproj(x, lw["norm_pre"], lw["w_in"], cos_t, sin_t, n_rope=2, tm=TM_INPROJ)
        oa = _dilated_attn(proj, bsz, seq)
        hb, c1, n1, m1, buf = _mlstm_prompt(proj, lw, bsz, seq)
        pv = proj.reshape(bsz, seq, AB_IN)
        new_k = pv[:, seq - keep:, A_WIDTH:2 * A_WIDTH].reshape(bsz, keep, A_HEADS, A_HEAD_DIM)
        new_v = pv[:, seq - keep:, 2 * A_WIDTH:3 * A_WIDTH].reshape(bsz, keep, A_HEADS, A_HEAD_DIM)
        x = _ab_out(x, proj, lw, tm=TM_OUT, oa=oa, hb=hb)
        for j, o in enumerate((new_k, new_v, c1, n1, m1, buf)):
            outs[j].append(o)
        if li < len(c_layers):
            lc = c_layers[li]
            projc = _norm_inproj(x, lc["norm_pre"], lc["w_in"], zeros, zeros, n_rope=0, tm=TM_INPROJ)
            x, h_last, cbuf = _rglru_prompt(x, projc, lc, bsz, seq, TM_RGLRU)
            outs[6].append(h_last)
            outs[7].append(cbuf)
    return x.reshape(bsz, seq, D_MODEL), [jnp.stack(o, axis=0) for o in outs]


def _sample_trunk(x_sample, cache_k, cache_v, st, ab_layers, c_layers):
    bd = x_sample.shape[0]
    x = x_sample.reshape(bd, D_MODEL)
    cos_t, sin_t = _rope_tables(jnp.full((bd,), PAST_LEN, dtype=jnp.int32))
    zeros = jnp.zeros((bd, LANES), F32)
    outs = [[] for _ in range(8)]
    upds = []
    for li in range(len(ab_layers)):
        lw = ab_layers[li]
        proj = _norm_inproj(x, lw["norm_pre"], lw["w_in"], cos_t, sin_t, n_rope=2, tm=bd)
        qkv = proj[:, :3 * A_WIDTH].reshape(bd, 3 * A_HEADS, A_HEAD_DIM)
        oa = _decode_attn(qkv, cache_k, cache_v, li)
        dec, upd, n1, m1, buf = _mlstm_decode(proj, st[3][li], st[0], li, st[1][li], st[2][li], lw)
        upds.append(upd)
        new_k = qkv[:, A_HEADS:2 * A_HEADS].reshape(bd, 1, A_HEADS, A_HEAD_DIM)
        new_v = qkv[:, 2 * A_HEADS:].reshape(bd, 1, A_HEADS, A_HEAD_DIM)
        x = _ab_out(x, proj, lw, tm=bd, oa=oa.reshape(bd, A_WIDTH), dec=dec)
        for j, o in zip((0, 1, 3, 4, 5), (new_k, new_v, n1, m1, buf)):
            outs[j].append(o)
        if li < len(c_layers):
            lc = c_layers[li]
            projc = _norm_inproj(x, lc["norm_pre"], lc["w_in"], zeros, zeros, n_rope=0, tm=bd)
            x, h, cbuf = _rglru_decode(x, projc, st[5][li], st[4][li], lc)
            outs[6].append(h)
            outs[7].append(cbuf)
    stacked = [None if j == 2 else jnp.stack(o, axis=0) for j, o in enumerate(outs)]
    stacked[2] = _mlstm_decode_cnew(st[0], upds)
    return x.reshape(bd, 1, D_MODEL), stacked


def kernel(x_prompt, x_sample, cache_k, cache_v, state_mlstm_c, state_mlstm_n, state_mlstm_m, state_mlstm_conv,
           state_rglru_h, state_rglru_conv, ab_norm_pre, ab_norm_post, ab_w_in, ab_w_out, mlstm_conv_w,
           mlstm_conv_b, mlstm_wq, mlstm_wk, mlstm_wv, mlstm_w_gates, mlstm_b_gates, mlstm_out_norm, mlstm_skip,
           c_norm_pre, c_norm_post, c_w_in, c_w_out, c_conv_w, c_conv_b, c_w_a, c_b_a, c_w_x, c_b_x, c_lambda):
    n_ab = ab_w_in.shape[0]
    n_c = c_w_in.shape[0]
    ab_layers = [_prep_ab(li, ab_norm_pre, ab_norm_post, ab_w_in, ab_w_out, mlstm_conv_w, mlstm_conv_b, mlstm_wq,
                          mlstm_wk, mlstm_wv, mlstm_w_gates, mlstm_b_gates, mlstm_out_norm, mlstm_skip)
                 for li in range(n_ab)]
    c_layers = [_prep_c(li, c_norm_pre, c_norm_post, c_w_in, c_w_out, c_conv_w, c_conv_b, c_w_a, c_b_a, c_w_x,
                        c_b_x, c_lambda) for li in range(n_c)]
    y_prompt, pst = _prompt_trunk(x_prompt, ab_layers, c_layers)
    st = (state_mlstm_c, state_mlstm_n, state_mlstm_m, state_mlstm_conv, state_rglru_h, state_rglru_conv)
    y_sample, sst = _sample_trunk(x_sample, cache_k, cache_v, st, ab_layers, c_layers)
    return (y_prompt, y_sample, *pst, *sst)
```

```python
import functools

import jax
import jax.numpy as jnp
from jax import lax
from jax.experimental import pallas as pl
from jax.experimental.pallas import tpu as pltpu

F32 = jnp.float32
BF16 = jnp.bfloat16

D_MODEL = 1024
PAST_LEN = 2048
A_HEADS = 8
A_HEAD_DIM = 128
A_WIDTH = 1024
A_PATTERNS = ((128, 1), (512, 4), (2048, 16))
A_MAX_WINDOW = 2048
ROPE_THETA = 10000.0
B_HEADS = 4
B_WIDTH = 1024
B_HEAD_DIM = 256
B_QKV_BLOCK = 4
B_CHUNK = 128
C_WIDTH = 1536
C_BLOCKS = 12
C_BLOCK = 128
RG_LRU_C = 8.0
CONV_WIDTH = 4
AB_IN = 6 * 1024
C_IN = 2 * C_WIDTH
NORM_EPS = 1e-6

LANES = 128
SUBLANES = 8
VMEM_LIMIT = 56 * 1024 * 1024

TM_INPROJ = 256
TM_RGLRU = 256
ATT_TQ = A_MAX_WINDOW
ATT_BLK = 128
NB_DECODE_ATTN = 4
NB_DECODE_STATE = 8


def _cparams(*sem):
    return pltpu.CompilerParams(dimension_semantics=sem, vmem_limit_bytes=VMEM_LIMIT)


def _sigmoid(x):
    return 1.0 / (1.0 + jnp.exp(-x))


def _silu(x):
    return x * _sigmoid(x)


def _log_sigmoid(x):
    return jnp.minimum(x, 0.0) - jnp.log1p(jnp.exp(-jnp.abs(x)))


def _softplus(x):
    return jnp.maximum(x, 0.0) + jnp.log1p(jnp.exp(-jnp.abs(x)))


def _rmsnorm(x, g):
    return x * lax.rsqrt(jnp.mean(x * x, axis=-1, keepdims=True) + NORM_EPS) * g


def _dot(a, b):
    return jnp.dot(a, b, preferred_element_type=F32)


def _dot_nt(a, b):
    return lax.dot_general(a, b, (((1,), (1,)), ((), ())), preferred_element_type=F32)


def _dot_tn(a, b):
    return lax.dot_general(a, b, (((0,), (0,)), ((), ())), preferred_element_type=F32)


def _inproj_kernel(x_ref, g_ref, w_ref, cos_ref, sin_ref, o_ref, *, n_rope):
    h = _rmsnorm(x_ref[...], g_ref[...]).astype(BF16)
    group = A_WIDTH
    for j in range(w_ref.shape[1] // group):
        p = _dot(h, w_ref[:, j * group:(j + 1) * group])
        if j < n_rope:
            c = cos_ref[...]
            s = sin_ref[...]
            for hd in range(group // A_HEAD_DIM):
                ph = p[:, hd * A_HEAD_DIM:(hd + 1) * A_HEAD_DIM]
                lo = j * group + hd * A_HEAD_DIM
                o_ref[:, lo:lo + A_HEAD_DIM] = ph * c + pltpu.roll(ph, A_HEAD_DIM // 2, 1) * s
        else:
            o_ref[:, j * group:(j + 1) * group] = p


def _norm_inproj(x, g, w_bf16, cos_t, sin_t, *, n_rope, tm):
    t, d = x.shape
    n = w_bf16.shape[1]
    n_pos_tiles = cos_t.shape[0] // tm
    return pl.pallas_call(
        functools.partial(_inproj_kernel, n_rope=n_rope),
        grid=(t // tm,),
        in_specs=[
            pl.BlockSpec((tm, d), lambda i: (i, 0)),
            pl.BlockSpec((1, d), lambda i: (0, 0)),
            pl.BlockSpec((d, n), lambda i: (0, 0)),
            pl.BlockSpec((tm, LANES), lambda i: (i % n_pos_tiles, 0)),
            pl.BlockSpec((tm, LANES), lambda i: (i % n_pos_tiles, 0)),
        ],
        out_specs=pl.BlockSpec((tm, n), lambda i: (i, 0)),
        out_shape=jax.ShapeDtypeStruct((t, n), F32),
        compiler_params=_cparams("parallel"),
        name="norm_inproj",
    )(x, g.reshape(1, d), w_bf16, cos_t, sin_t)


def _attend(q, k, v, bias):
    sc = _dot_nt(q, k) * (A_HEAD_DIM ** -0.5) + bias
    m = jnp.max(sc, axis=-1, keepdims=True)
    p = jnp.exp(sc - m)
    l = jnp.sum(p, axis=-1, keepdims=True)
    o = _dot(p.astype(BF16), v) / l
    return o, jnp.broadcast_to(m + jnp.log(l), (ATT_BLK, LANES))


def _attn_kernel(q_ref, k_ref, v_ref, o_ref, d4f, q1, q4, q16, k1e, k4e, k16e, v1e, v4e, v16e,
                 o1n, o4n, o16n, l1n, l4n, l16n):
    i = pl.program_id(2)
    blk = ATT_BLK
    tq = ATT_TQ
    n4 = tq // 4
    n16 = tq // 16

    @pl.when(i == 0)
    def _():
        for r in (k1e, v1e):
            r[0:blk, :] = jnp.zeros((blk, LANES), BF16)
        for r in (k4e, v4e, k16e, v16e):
            r[:, 0:blk, :] = jnp.zeros((r.shape[0], blk, LANES), BF16)

    @pl.when(i > 0)
    def _():
        for r in (k1e, v1e):
            r[0:blk, :] = r[tq:tq + blk, :]
        for r in (k4e, v4e):
            r[:, 0:blk, :] = r[:, n4:n4 + blk, :]
        for r in (k16e, v16e):
            r[:, 0:blk, :] = r[:, n16:n16 + blk, :]

    def split(src_ref, d1, d4, d16, off):
        def natural(c, carry):
            r0 = pl.multiple_of(c * 2 * blk, 2 * blk)
            d1[pl.ds(off + r0, 2 * blk), :] = src_ref[0, pl.ds(r0, 2 * blk), :].astype(BF16)
            return carry
        lax.fori_loop(0, tq // (2 * blk), natural, 0)
        for r4 in range(4):
            for c in range(n4 // blk):
                x = src_ref[0, pl.ds(r4 + 4 * c * blk, blk, stride=4), :]
                d4f[r4 * n4 + c * blk:r4 * n4 + (c + 1) * blk, :] = x
                d4[r4, off + c * blk:off + (c + 1) * blk, :] = x.astype(BF16)
        for r4 in range(4):
            for a in range(4):
                x = d4f[pl.ds(r4 * n4 + a, n16, stride=4), :]
                d16[r4 + 4 * a, off:off + n16, :] = x.astype(BF16)

    split(q_ref, q1, q4, q16, 0)
    split(k_ref, k1e, k4e, k16e, blk)
    split(v_ref, v1e, v4e, v16e, blk)

    row = lax.broadcasted_iota(jnp.int32, (blk, 2 * blk), 0)
    col = lax.broadcasted_iota(jnp.int32, (blk, 2 * blk), 1)
    bias = jnp.where((col >= row) & (col - row <= blk), 0.0, -jnp.inf)
    no_prev = jnp.where(i == 0, blk, 0)
    bias_first = jnp.where(col >= no_prev, bias, -jnp.inf)

    for s in range(tq // blk):
        rows = slice(s * blk, (s + 1) * blk)
        win = slice(s * blk, (s + 2) * blk)
        o, lse = _attend(q1[rows, :], k1e[win, :], v1e[win, :], bias_first if s == 0 else bias)
        o1n[rows, :] = o
        l1n[rows, :] = lse

    for r4 in range(4):
        for s in range(n4 // blk):
            win = slice(s * blk, (s + 2) * blk)
            o, lse = _attend(q4[r4, s * blk:(s + 1) * blk, :], k4e[r4, win, :], v4e[r4, win, :],
                             bias_first if s == 0 else bias)
            o4n[pl.ds(4 * s * blk + r4, blk, stride=4), :] = o
            l4n[pl.ds(4 * s * blk + r4, blk, stride=4), :] = lse

    for r in range(16):
        o, lse = _attend(q16[r], k16e[r], v16e[r], bias_first)
        o16n[pl.ds(r, blk, stride=16), :] = o
        l16n[pl.ds(r, blk, stride=16), :] = lse

    def merge(c, carry):
        r0 = pl.multiple_of(c * blk, blk)
        rows = pl.ds(r0, blk)
        l1, l4, l16 = l1n[rows, :], l4n[rows, :], l16n[rows, :]
        mx = jnp.maximum(jnp.maximum(l1, l4), l16)
        e1, e4, e16 = jnp.exp(l1 - mx), jnp.exp(l4 - mx), jnp.exp(l16 - mx)
        o_ref[0, rows, :] = (o1n[rows, :] * e1 + o4n[rows, :] * e4 + o16n[rows, :] * e16) / (e1 + e4 + e16)
        return carry
    lax.fori_loop(0, tq // blk, merge, 0)


def _dilated_attn(proj, bsz, seq):
    tq, blk = ATT_TQ, ATT_BLK
    view = proj.reshape(bsz, seq, AB_IN)
    col = lambda c: pl.BlockSpec((1, tq, A_HEAD_DIM), lambda b, h, i: (b, i, c * A_HEADS + h))
    bf = lambda *s: pltpu.VMEM(s, BF16)
    nat = pltpu.VMEM((tq, LANES), F32)
    out = pl.pallas_call(
        _attn_kernel,
        grid=(bsz, A_HEADS, seq // tq),
        in_specs=[col(0), col(1), col(2)],
        out_specs=pl.BlockSpec((1, tq, A_HEAD_DIM), lambda b, h, i: (b, i, h)),
        out_shape=jax.ShapeDtypeStruct((bsz, seq, A_WIDTH), F32),
        scratch_shapes=[nat,
                        bf(tq, LANES), bf(4, tq // 4, LANES), bf(16, tq // 16, LANES),
                        bf(tq + blk, LANES), bf(4, tq // 4 + blk, LANES), bf(16, tq // 16 + blk, LANES),
                        bf(tq + blk, LANES), bf(4, tq // 4 + blk, LANES), bf(16, tq // 16 + blk, LANES),
                        nat, nat, nat, nat, nat, nat],
        compiler_params=_cparams("parallel", "parallel", "arbitrary"),
        name="dilated_attn",
    )(view, view, view)
    return out.reshape(bsz * seq, A_WIDTH)


def _decode_attn_kernel(q_ref, kn_ref, vn_ref, k1_ref, k4_ref, k16_ref, v1_ref, v4_ref, v16_ref, o_ref, *, nb):
    scale = A_HEAD_DIM ** -0.5
    for bb in range(nb):
        q = q_ref[bb]
        kn = kn_ref[bb]
        vn = vn_ref[bb]
        s_new = jnp.sum(q * kn, axis=-1, keepdims=True) * scale
        outs, lses = [], []
        for k_ref, v_ref in ((k1_ref, v1_ref), (k4_ref, v4_ref), (k16_ref, v16_ref)):
            kk = k_ref[0, bb, :, 0]
            vv = v_ref[0, bb, :, 0]
            sc = jnp.sum(kk * q[None], axis=-1, keepdims=True) * scale
            m = jnp.maximum(jnp.max(sc, axis=0), s_new)
            p = jnp.exp(sc - m[None])
            p_new = jnp.exp(s_new - m)
            l = jnp.sum(p, axis=0) + p_new
            outs.append((jnp.sum(p * vv, axis=0) + p_new * vn) / l)
            lses.append(m + jnp.log(l))
        mx = jnp.maximum(jnp.maximum(lses[0], lses[1]), lses[2])
        e = [jnp.exp(x - mx) for x in lses]
        tot = e[0] + e[1] + e[2]
        o_ref[bb] = (outs[0] * e[0] + outs[1] * e[1] + outs[2] * e[2]) / tot


def _decode_attn(qkv, cache_k, cache_v, li):
    bd = qkv.shape[0]
    n_layers, _, win = cache_k.shape[:3]
    nb = NB_DECODE_ATTN
    new = lambda c: pl.BlockSpec((nb, A_HEADS, A_HEAD_DIM), lambda i: (i, c, 0))
    specs, views = [], []
    for cache in (cache_k, cache_v):
        for window, dil in A_PATTERNS:
            rows = win // dil
            n_keys = window // dil
            views.append(cache.reshape(n_layers, bd, rows, dil, A_HEADS, A_HEAD_DIM))
            specs.append(pl.BlockSpec((1, nb, n_keys, 1, A_HEADS, A_HEAD_DIM),
                                      lambda i, rb=rows // n_keys - 1: (li, i, rb, 0, 0, 0)))
    return pl.pallas_call(
        functools.partial(_decode_attn_kernel, nb=nb),
        grid=(bd // nb,),
        in_specs=[new(0), new(1), new(2)] + specs,
        out_specs=pl.BlockSpec((nb, A_HEADS, A_HEAD_DIM), lambda i: (i, 0, 0)),
        out_shape=jax.ShapeDtypeStruct((bd, A_HEADS, A_HEAD_DIM), F32),
        compiler_params=_cparams("parallel"),
        name="decode_attn",
    )(qkv, qkv, qkv, *views)


def _blockdiag_dense(w):
    nb = w.shape[0]
    per = LANES // B_QKV_BLOCK
    w4 = w.reshape(nb // per, per, B_QKV_BLOCK, B_QKV_BLOCK)
    return jnp.einsum('gnio,nm->gnimo', w4, jnp.eye(per, dtype=w.dtype)).reshape(nb // per, LANES, LANES)


def _hi_lo(x):
    hi = x.astype(BF16)
    return hi, (x - hi.astype(F32)).astype(BF16)


def _blockdiag_mxu(x, whi_ref, wlo_ref):
    outs = []
    for g in range(x.shape[1] // LANES):
        hi, lo = _hi_lo(x[:, g * LANES:(g + 1) * LANES])
        outs.append(_dot(hi, whi_ref[g]) + _dot(lo, whi_ref[g]) + _dot(hi, wlo_ref[g]))
    return outs


def _mlstm_qkv(xb, xc, wqk_hi, wqk_lo, wv_hi, wv_lo):
    qk = _blockdiag_mxu(xc, wqk_hi, wqk_lo)
    qb = jnp.concatenate([o[:, :LANES] for o in qk], axis=1)
    kb = jnp.concatenate([o[:, LANES:] for o in qk], axis=1)
    vb = jnp.concatenate(_blockdiag_mxu(xb, wv_hi, wv_lo), axis=1)
    return qb, kb, vb


def _head_out(hb_raw, xc, on_ref, sk_ref):
    parts = []
    for h in range(B_HEADS):
        xh = hb_raw[:, h * B_HEAD_DIM:(h + 1) * B_HEAD_DIM]
        xcn = xh - jnp.mean(xh, axis=-1, keepdims=True)
        parts.append(xcn * lax.rsqrt(jnp.mean(xcn * xcn, axis=-1, keepdims=True) + 1e-5))
    return jnp.concatenate(parts, axis=1) * on_ref[...] + sk_ref[...] * xc


def _cumsum_lanes(x):
    t = x.shape[1]
    idx = lax.broadcasted_iota(jnp.int32, x.shape, 1)
    k = 1
    while k < t:
        x = x + jnp.where(idx >= k, pltpu.roll(x, k, 1), 0.0)
        k *= 2
    return x


def _mlstm_chunk(x, xx_scr, hraw_scr, c_scr, n_scr, m_scr, cw_ref, cb_ref, wqk_hi, wqk_lo, wv_hi, wv_lo,
                 wgt_ref, bgt_ref, on_ref, sk_ref):
    t = B_CHUNK
    halo = SUBLANES
    ng = 2 * B_HEADS
    xx_scr[halo:halo + t, :] = x
    xconv = x * cw_ref[3:4, :] + cb_ref[...]
    for s in range(1, CONV_WIDTH):
        xconv = xconv + xx_scr[pl.ds(halo - s, t), :] * cw_ref[3 - s:4 - s, :]
    xx_scr[0:halo, :] = x[t - halo:t, :]
    xc = _silu(xconv)

    qb, kb, vb = _mlstm_qkv(x, xc, wqk_hi, wqk_lo, wv_hi, wv_lo)
    qh, kh, vh = qb.astype(BF16), kb.astype(BF16), vb.astype(BF16)
    gt = (_dot_nt(wgt_ref[:, 0:B_WIDTH], qh) + _dot_nt(wgt_ref[:, B_WIDTH:2 * B_WIDTH], kh)
          + _dot_nt(wgt_ref[:, 2 * B_WIDTH:3 * B_WIDTH], vh) + bgt_ref[...])
    brow = _cumsum_lanes(_log_sigmoid(gt))
    gcol = jnp.concatenate([gt, brow, jnp.zeros((t - 2 * ng, t), F32)], axis=0).T
    ri = lax.broadcasted_iota(jnp.int32, (t, t), 0)
    cj = lax.broadcasted_iota(jnp.int32, (t, t), 1)
    causal = cj <= ri
    kscale = B_HEAD_DIM ** -0.5
    for h in range(B_HEADS):
        sl = slice(h * B_HEAD_DIM, (h + 1) * B_HEAD_DIM)
        q = qb[:, sl]
        k = kb[:, sl] * kscale
        qhh = qh[:, sl]
        khh = k.astype(BF16)
        vhh = vh[:, sl]
        logi_c = gcol[:, h:h + 1]
        b_c = gcol[:, ng + B_HEADS + h:ng + B_HEADS + h + 1]
        logi_r = gt[h:h + 1, :]
        b_r = brow[B_HEADS + h:B_HEADS + h + 1, :]
        m0 = m_scr[h:h + 1, 0:1]
        n0 = n_scr[h:h + 1, :]
        c0 = c_scr[h]
        dmat = jnp.where(causal, b_c - b_r + logi_r, -jnp.inf)
        inter = b_c + m0
        m = jnp.maximum(jnp.max(dmat, axis=1, keepdims=True), inter)
        w_intra = jnp.exp(dmat - m)
        w_inter = jnp.exp(inter - m)
        s = _dot_nt(qhh, khh) * w_intra
        num = _dot(s.astype(BF16), vhh) + w_inter * _dot(qhh, c0.astype(BF16))
        den = jnp.sum(s, axis=1, keepdims=True) + w_inter * jnp.sum(q * n0, axis=1, keepdims=True)
        hraw_scr[:, sl] = num / jnp.maximum(jnp.abs(den), jnp.exp(-m))
        bl = b_c[t - 1:t, :]
        dl = bl - b_c + logi_c
        m_new = jnp.maximum(bl + m0, jnp.max(dl, axis=0, keepdims=True))
        kw = jnp.exp(dl - m_new) * k
        carry = jnp.exp(bl + m0 - m_new)
        c_scr[h] = carry * c0 + _dot_tn(kw.astype(BF16), vhh)
        n_scr[h:h + 1, :] = carry * n0 + jnp.sum(kw, axis=0, keepdims=True)
        m_scr[h:h + 1, :] = jnp.broadcast_to(m_new, (1, LANES))
    return _head_out(hraw_scr[...], xc, on_ref, sk_ref)


def _gate_project(oa, hb, za, zb, x, w_ref, g_ref):
    ya = oa * _silu(za)
    yb = hb * _silu(zb)
    y = _dot(ya.astype(BF16), w_ref[0:A_WIDTH, :]) + _dot(yb.astype(BF16), w_ref[A_WIDTH:A_WIDTH + B_WIDTH, :])
    return x + _rmsnorm(y, g_ref[...])


def _mlstm_prompt_kernel(xb_ref, oa_ref, za_ref, zb_ref, x_ref, cw_ref, cb_ref, wqk_hi, wqk_lo, wv_hi, wv_lo,
                         wgt_ref, bgt_ref, on_ref, sk_ref, w_ref, g_ref,
                         o_ref, c_out, n_out, m_out, buf_out,
                         xx_scr, hraw_scr, c_scr, n_scr, m_scr):
    ci = pl.program_id(0)
    nc = pl.num_programs(0)
    t = B_CHUNK
    halo = SUBLANES
    bsz = xb_ref.shape[0]

    @pl.when(ci == 0)
    def _():
        xx_scr[:, 0:halo, :] = jnp.zeros((bsz, halo, B_WIDTH), F32)
        c_scr[...] = jnp.zeros_like(c_scr)
        n_scr[...] = jnp.zeros_like(n_scr)
        m_scr[...] = jnp.zeros_like(m_scr)

    for b in range(bsz):
        hb = _mlstm_chunk(xb_ref[b], xx_scr.at[b], hraw_scr.at[b], c_scr.at[b], n_scr.at[b], m_scr.at[b],
                          cw_ref, cb_ref, wqk_hi, wqk_lo, wv_hi, wv_lo, wgt_ref, bgt_ref, on_ref, sk_ref)
        o_ref[b] = _gate_project(oa_ref[b], hb, za_ref[b], zb_ref[b], x_ref[b], w_ref, g_ref)

    @pl.when(ci == nc - 1)
    def _():
        c_out[...] = c_scr[...]
        n_out[...] = n_scr[:, 0:B_HEADS, :]
        lane = lax.broadcasted_iota(jnp.int32, (1, LANES), 1)
        for b in range(bsz):
            mt = jnp.zeros((1, LANES), F32)
            for h in range(B_HEADS):
                mt = jnp.where(lane == h, m_scr[b, h:h + 1, :], mt)
            m_out[b] = mt
            buf_out[b] = xb_ref[b, t - halo:t, :]


def _mlstm_prompt(x, proj, oa, lw, bsz, seq):
    t = B_CHUNK
    view = proj.reshape(bsz, seq, AB_IN)
    full = lambda a: pl.BlockSpec(a.shape, lambda c: (0,) * a.ndim)
    pcol = lambda col: pl.BlockSpec((bsz, t, B_WIDTH), lambda c: (0, c, col))
    row = pl.BlockSpec((bsz, t, D_MODEL), lambda c: (0, c, 0))
    whole = lambda *s: pl.BlockSpec(s, lambda c: (0,) * len(s))
    params = [lw["conv_w"], lw["conv_b"], lw["wqk_hi"], lw["wqk_lo"], lw["wv_hi"], lw["wv_lo"], lw["wgt"],
              lw["bgt"], lw["out_norm"], lw["skip"], lw["w_out"], lw["norm_post"]]
    sds = jax.ShapeDtypeStruct
    y, c, n, m, buf = pl.pallas_call(
        _mlstm_prompt_kernel,
        grid=(seq // t,),
        in_specs=[pcol(4), row, pcol(3), pcol(5), row] + [full(a) for a in params],
        out_specs=[row,
                   whole(bsz, B_HEADS, B_HEAD_DIM, B_HEAD_DIM),
                   whole(bsz, B_HEADS, B_HEAD_DIM),
                   whole(bsz, 1, LANES),
                   whole(bsz, SUBLANES, B_WIDTH)],
        out_shape=[sds((bsz, seq, D_MODEL), F32),
                   sds((bsz, B_HEADS, B_HEAD_DIM, B_HEAD_DIM), F32),
                   sds((bsz, B_HEADS, B_HEAD_DIM), F32),
                   sds((bsz, 1, LANES), F32),
                   sds((bsz, SUBLANES, B_WIDTH), F32)],
        scratch_shapes=[pltpu.VMEM((bsz, t + SUBLANES, B_WIDTH), F32),
                        pltpu.VMEM((bsz, t, B_WIDTH), F32),
                        pltpu.VMEM((bsz, B_HEADS, B_HEAD_DIM, B_HEAD_DIM), F32),
                        pltpu.VMEM((bsz, SUBLANES, B_HEAD_DIM), F32),
                        pltpu.VMEM((bsz, SUBLANES, LANES), F32)],
        compiler_params=_cparams("arbitrary"),
        name="mlstm_prompt",
    )(view, oa.reshape(bsz, seq, A_WIDTH), view, view, x.reshape(bsz, seq, D_MODEL), *params)
    return (y.reshape(bsz * seq, D_MODEL), c, n, m[:, 0, :B_HEADS], buf[:, SUBLANES - (CONV_WIDTH - 1):, :])


def _mlstm_decode_pre_kernel(xb_ref, b0_ref, b1_ref, b2_ref, cw_ref, cb_ref, wqk_hi, wqk_lo, wv_hi, wv_lo, wg_ref,
                             bg_ref, n0_ref, m0_ref,
                             xc_ref, v_ref, sv_ref, wi_ref, den_ref, n_out, m_out, buf_out, qt_ref, kwt_ref):
    x = xb_ref[...]
    xconv = (x * cw_ref[3:4, :] + b2_ref[...] * cw_ref[2:3, :] + b1_ref[...] * cw_ref[1:2, :]
             + b0_ref[...] * cw_ref[0:1, :] + cb_ref[...])
    xc = _silu(xconv)
    xc_ref[...] = xc
    buf_out[:, 0:B_WIDTH] = b1_ref[...]
    buf_out[:, B_WIDTH:2 * B_WIDTH] = b2_ref[...]
    buf_out[:, 2 * B_WIDTH:3 * B_WIDTH] = x
    qb, kb, vb = _mlstm_qkv(x, xc, wqk_hi, wqk_lo, wv_hi, wv_lo)
    g = (_dot(qb.astype(BF16), wg_ref[0:B_WIDTH, :]) + _dot(kb.astype(BF16), wg_ref[B_WIDTH:2 * B_WIDTH, :])
         + _dot(vb.astype(BF16), wg_ref[2 * B_WIDTH:3 * B_WIDTH, :]) + bg_ref[...])
    v_ref[...] = vb
    logf = _log_sigmoid(g)
    rows = x.shape[0]
    lane = lax.broadcasted_iota(jnp.int32, (rows, LANES), 1)
    m_tile = jnp.zeros((rows, LANES), F32)
    kscale = B_HEAD_DIM ** -0.5
    for h in range(B_HEADS):
        sl = slice(h * B_HEAD_DIM, (h + 1) * B_HEAD_DIM)
        logi = g[:, h:h + 1]
        fm = logf[:, B_HEADS + h:B_HEADS + h + 1] + m0_ref[:, h:h + 1]
        m = jnp.maximum(logi, fm)
        w_intra = jnp.exp(logi - m)
        w_inter = jnp.exp(fm - m)
        q = qb[:, sl]
        k = kb[:, sl] * kscale
        n0 = n0_ref[:, sl]
        s = jnp.sum(q * k, axis=1, keepdims=True) * w_intra
        den = s + w_inter * jnp.sum(q * n0, axis=1, keepdims=True)
        kw = w_intra * k
        sv_ref[:, sl] = s * vb[:, sl]
        wi_ref[:, sl] = jnp.broadcast_to(w_inter, q.shape)
        den_ref[:, sl] = jnp.broadcast_to(jnp.maximum(jnp.abs(den), jnp.exp(-m)), q.shape)
        n_out[:, sl] = w_inter * n0 + kw
        m_tile = jnp.where(lane == h, m, m_tile)
        qt_ref[h] = q.T
        kwt_ref[h] = kw.T
    m_out[...] = m_tile


def _column(mat_t, b):
    lane = lax.broadcasted_iota(jnp.int32, mat_t.shape, 1)
    return jnp.sum(jnp.where(lane == b, mat_t, 0.0), axis=1, keepdims=True)


def _mlstm_decode_qc_kernel(c0_ref, qt_ref, qc_ref, *, nb):
    i = pl.program_id(1)
    qt = qt_ref[0]
    for bb in range(nb):
        qc_ref[bb] = jnp.sum(c0_ref[0, bb, 0] * _column(qt, i * nb + bb), axis=0, keepdims=True)


def _mlstm_decode_cnew_kernel(c0_ref, kwt_ref, v_ref, wi_ref, c_out, *, nb):
    i = pl.program_id(2)
    kwt = kwt_ref[0, 0]
    for bb in range(nb):
        c_out[0, bb, 0] = wi_ref[0, bb] * c0_ref[0, bb, 0] + _column(kwt, i * nb + bb) * v_ref[0, bb]


def _mlstm_decode(proj, conv_state, state_c, li, n0, m0, lw):
    bd = proj.shape[0]
    w = B_WIDTH
    cs = conv_state.reshape(bd, (CONV_WIDTH - 1) * w)
    m0p = jnp.pad(m0, ((0, 0), (0, LANES - B_HEADS)))
    row = lambda c: pl.BlockSpec((bd, w), lambda i: (0, c))
    full = lambda a: pl.BlockSpec(a.shape, lambda i: (0,) * a.ndim)
    params = [lw["conv_w"], lw["conv_b"], lw["wqk_hi"], lw["wqk_lo"], lw["wv_hi"], lw["wv_lo"], lw["wg"], lw["bg"]]
    n0f = n0.reshape(bd, w)
    sds = jax.ShapeDtypeStruct
    tr = pl.BlockSpec((B_HEADS, B_HEAD_DIM, bd), lambda i: (0, 0, 0))
    outs = pl.pallas_call(
        _mlstm_decode_pre_kernel,
        grid=(1,),
        in_specs=[row(4), row(0), row(1), row(2)] + [full(a) for a in params] + [full(n0f), full(m0p)],
        out_specs=[row(0)] * 6 + [pl.BlockSpec((bd, LANES), lambda i: (0, 0)),
                                  pl.BlockSpec((bd, 3 * w), lambda i: (0, 0)), tr, tr],
        out_shape=[sds((bd, w), F32)] * 6 + [sds((bd, LANES), F32), sds((bd, 3 * w), F32),
                                             sds((B_HEADS, B_HEAD_DIM, bd), F32),
                                             sds((B_HEADS, B_HEAD_DIM, bd), F32)],
        compiler_params=_cparams("arbitrary"),
        name="mlstm_decode_pre",
    )(proj, cs, cs, cs, *params, n0f, m0p)
    xc, v, sv, wi, den, n_new, m_new, buf, qt, kwt = outs
    nb = NB_DECODE_STATE
    qc = pl.pallas_call(
        functools.partial(_mlstm_decode_qc_kernel, nb=nb),
        grid=(B_HEADS, bd // nb),
        in_specs=[pl.BlockSpec((1, nb, 1, B_HEAD_DIM, B_HEAD_DIM), lambda h, i: (li, i, h, 0, 0)),
                  pl.BlockSpec((1, B_HEAD_DIM, bd), lambda h, i: (h, 0, 0))],
        out_specs=pl.BlockSpec((nb, 1, B_HEAD_DIM), lambda h, i: (i, 0, h)),
        out_shape=sds((bd, 1, w), F32),
        compiler_params=_cparams("parallel", "arbitrary"),
        name="mlstm_decode_qc",
    )(state_c, qt)
    dec = dict(xc=xc, sv=sv, wi=wi, den=den, qc=qc.reshape(bd, w))
    upd = dict(kwt=kwt, v=v.reshape(bd, 1, w), wi=wi.reshape(bd, 1, w))
    return (dec, upd, n_new.reshape(bd, B_HEADS, B_HEAD_DIM), m_new[:, :B_HEADS],
            buf.reshape(bd, CONV_WIDTH - 1, w))


def _mlstm_decode_cnew(state_c, upds):
    n_layers, bd = state_c.shape[:2]
    nb = NB_DECODE_STATE
    kwt = jnp.stack([u["kwt"] for u in upds], axis=0)
    v = jnp.stack([u["v"] for u in upds], axis=0)
    wi = jnp.stack([u["wi"] for u in upds], axis=0)
    cblk = pl.BlockSpec((1, nb, 1, B_HEAD_DIM, B_HEAD_DIM), lambda l, h, i: (l, i, h, 0, 0))
    rowh = pl.BlockSpec((1, nb, 1, B_HEAD_DIM), lambda l, h, i: (l, i, 0, h))
    return pl.pallas_call(
        functools.partial(_mlstm_decode_cnew_kernel, nb=nb),
        grid=(n_layers, B_HEADS, bd // nb),
        in_specs=[cblk, pl.BlockSpec((1, 1, B_HEAD_DIM, bd), lambda l, h, i: (l, h, 0, 0)), rowh, rowh],
        out_specs=cblk,
        out_shape=jax.ShapeDtypeStruct(state_c.shape, F32),
        compiler_params=_cparams("parallel", "parallel", "arbitrary"),
        name="mlstm_decode_cnew",
    )(state_c, kwt, v, wi)


def _ab_out_decode_kernel(oa_ref, sv_ref, wi_ref, qc_ref, den_ref, xc_ref, za_ref, zb_ref, x_ref,
                          w_ref, on_ref, sk_ref, g_ref, o_ref):
    hb = _head_out((sv_ref[...] + wi_ref[...] * qc_ref[...]) / den_ref[...], xc_ref[...], on_ref, sk_ref)
    o_ref[...] = _gate_project(oa_ref[...], hb, za_ref[...], zb_ref[...], x_ref[...], w_ref, g_ref)


def _ab_out_decode(x, proj, lw, oa, dec):
    bd, w = x.shape
    row = pl.BlockSpec((bd, w), lambda i: (0, 0))
    pcol = lambda c: pl.BlockSpec((bd, w), lambda i: (0, c))
    full = lambda a: pl.BlockSpec(a.shape, lambda i: (0,) * a.ndim)
    params = [lw["w_out"], lw["out_norm"], lw["skip"], lw["norm_post"]]
    return pl.pallas_call(
        _ab_out_decode_kernel,
        grid=(1,),
        in_specs=[row] * 6 + [pcol(3), pcol(5), row] + [full(a) for a in params],
        out_specs=row,
        out_shape=jax.ShapeDtypeStruct((bd, w), F32),
        compiler_params=_cparams("arbitrary"),
        name="ab_out_decode",
    )(oa, dec["sv"], dec["wi"], dec["qc"], dec["den"], dec["xc"], proj, proj, x, *params)


def _rglru_gates(xconv, wa_ref, ba_ref, wx_ref, bx_ref, lam_ref):
    rs, is_ = [], []
    for n in range(C_BLOCKS):
        sl = slice(n * C_BLOCK, (n + 1) * C_BLOCK)
        xa = xconv[:, sl].astype(BF16)
        rs.append(_dot(xa, wa_ref[n]))
        is_.append(_dot(xa, wx_ref[n]))
    r = _sigmoid(jnp.concatenate(rs, axis=1) + ba_ref[...])
    ig = _sigmoid(jnp.concatenate(is_, axis=1) + bx_ref[...])
    log_a = -RG_LRU_C * r * _softplus(-lam_ref[...])
    a = jnp.exp(log_a)
    u = jnp.sqrt(1.0 - a * a) * (ig * xconv)
    return a, u


def _rglru_out(h, z, x, w_ref, g_ref):
    y = _dot((h * _silu(z)).astype(BF16), w_ref[...])
    return x + _rmsnorm(y, g_ref[...])


def _rglru_prompt_kernel(xc_ref, zc_ref, x_ref, cw_ref, cb_ref, wa_ref, ba_ref, wx_ref, bx_ref, lam_ref,
                         w_ref, g_ref, o_ref, h_out, buf_out, xx_scr, a_scr, u_scr, hc_scr, *, tm):
    ti = pl.program_id(1)
    nt = pl.num_programs(1)
    halo = SUBLANES

    @pl.when(ti == 0)
    def _():
        xx_scr[0:halo, :] = jnp.zeros((halo, C_WIDTH), F32)
        hc_scr[...] = jnp.zeros_like(hc_scr)

    x = xc_ref[0]
    xx_scr[halo:halo + tm, :] = x
    xconv = x * cw_ref[3:4, :] + cb_ref[...]
    for s in range(1, CONV_WIDTH):
        xconv = xconv + xx_scr[pl.ds(halo - s, tm), :] * cw_ref[3 - s:4 - s, :]
    xx_scr[0:halo, :] = x[tm - halo:tm, :]
    a, u = _rglru_gates(xconv, wa_ref, ba_ref, wx_ref, bx_ref, lam_ref)
    a_scr[...] = a
    u_scr[...] = u
    ridx = lax.broadcasted_iota(jnp.int32, (SUBLANES, C_WIDTH), 0)

    def body(r, hprev):
        r8 = pl.multiple_of(r * SUBLANES, SUBLANES)
        aa = a_scr[pl.ds(r8, SUBLANES), :]
        uu = u_scr[pl.ds(r8, SUBLANES), :]
        for sft in (1, 2, 4):
            keep = ridx >= sft
            a_sh = jnp.where(keep, pltpu.roll(aa, sft, 0), 1.0)
            u_sh = jnp.where(keep, pltpu.roll(uu, sft, 0), 0.0)
            uu = aa * u_sh + uu
            aa = aa * a_sh
        hblk = aa * hprev + uu
        u_scr[pl.ds(r8, SUBLANES), :] = hblk
        return hblk[SUBLANES - 1:SUBLANES, :]

    hlast = lax.fori_loop(0, tm // SUBLANES, body, hc_scr[0:1, :])
    hc_scr[0:1, :] = hlast
    o_ref[0] = _rglru_out(u_scr[...], zc_ref[0], x_ref[0], w_ref, g_ref)

    @pl.when(ti == nt - 1)
    def _():
        h_out[0] = hlast
        buf_out[0] = x[tm - halo:tm, :]


def _rglru_decode_kernel(xc_ref, zc_ref, x_ref, b0_ref, b1_ref, b2_ref, h0_ref, cw_ref, cb_ref, wa_ref, ba_ref,
                         wx_ref, bx_ref, lam_ref, w_ref, g_ref, o_ref, h_out, buf_out):
    x = xc_ref[...]
    xconv = (x * cw_ref[3:4, :] + b2_ref[...] * cw_ref[2:3, :] + b1_ref[...] * cw_ref[1:2, :]
             + b0_ref[...] * cw_ref[0:1, :] + cb_ref[...])
    a, u = _rglru_gates(xconv, wa_ref, ba_ref, wx_ref, bx_ref, lam_ref)
    h = a * h0_ref[...] + u
    h_out[...] = h
    buf_out[:, 0:C_WIDTH] = b1_ref[...]
    buf_out[:, C_WIDTH:2 * C_WIDTH] = b2_ref[...]
    buf_out[:, 2 * C_WIDTH:3 * C_WIDTH] = x
    o_ref[...] = _rglru_out(h, zc_ref[...], x_ref[...], w_ref, g_ref)


def _c_params(lw):
    return [lw["conv_w"], lw["conv_b"], lw["w_a"], lw["b_a"], lw["w_x"], lw["b_x"], lw["lam"],
            lw["w_out"], lw["norm_post"]]


def _rglru_prompt(x, proj, lw, bsz, seq, tm):
    cw = C_WIDTH
    pv = proj.reshape(bsz, seq, C_IN)
    xv = x.reshape(bsz, seq, D_MODEL)
    params = _c_params(lw)
    full = lambda a: pl.BlockSpec(a.shape, lambda b, i: (0,) * a.ndim)
    sds = jax.ShapeDtypeStruct
    y, h_last, buf = pl.pallas_call(
        functools.partial(_rglru_prompt_kernel, tm=tm),
        grid=(bsz, seq // tm),
        in_specs=[pl.BlockSpec((1, tm, cw), lambda b, i: (b, i, 0)),
                  pl.BlockSpec((1, tm, cw), lambda b, i: (b, i, 1)),
                  pl.BlockSpec((1, tm, D_MODEL), lambda b, i: (b, i, 0))] + [full(a) for a in params],
        out_specs=[pl.BlockSpec((1, tm, D_MODEL), lambda b, i: (b, i, 0)),
                   pl.BlockSpec((1, 1, cw), lambda b, i: (b, 0, 0)),
                   pl.BlockSpec((1, SUBLANES, cw), lambda b, i: (b, 0, 0))],
        out_shape=[sds((bsz, seq, D_MODEL), F32), sds((bsz, 1, cw), F32), sds((bsz, SUBLANES, cw), F32)],
        scratch_shapes=[pltpu.VMEM((tm + SUBLANES, cw), F32), pltpu.VMEM((tm, cw), F32),
                        pltpu.VMEM((tm, cw), F32), pltpu.VMEM((SUBLANES, cw), F32)],
        compiler_params=_cparams("parallel", "arbitrary"),
        name="rglru_prompt",
    )(pv, pv, xv, *params)
    return y.reshape(bsz * seq, D_MODEL), h_last[:, 0, :], buf[:, SUBLANES - (CONV_WIDTH - 1):, :]


def _rglru_decode(x, proj, conv_state, h0, lw):
    bd = x.shape[0]
    cw = C_WIDTH
    cs = conv_state.reshape(bd, (CONV_WIDTH - 1) * cw)
    params = _c_params(lw)
    col = lambda c: pl.BlockSpec((bd, cw), lambda i: (0, c))
    full = lambda a: pl.BlockSpec(a.shape, lambda i: (0,) * a.ndim)
    sds = jax.ShapeDtypeStruct
    y, h, buf = pl.pallas_call(
        _rglru_decode_kernel,
        grid=(1,),
        in_specs=[col(0), col(1), full(x), col(0), col(1), col(2), full(h0)] + [full(a) for a in params],
        out_specs=[pl.BlockSpec((bd, D_MODEL), lambda i: (0, 0)), pl.BlockSpec((bd, cw), lambda i: (0, 0)),
                   pl.BlockSpec((bd, 3 * cw), lambda i: (0, 0))],
        out_shape=[sds((bd, D_MODEL), F32), sds((bd, cw), F32), sds((bd, 3 * cw), F32)],
        compiler_params=_cparams("arbitrary"),
        name="rglru_decode",
    )(proj, proj, x, cs, cs, cs, h0, *params)
    return y, h, buf.reshape(bd, CONV_WIDTH - 1, cw)


def _rope_tables(pos):
    half = A_HEAD_DIM // 2
    lane = jnp.arange(A_HEAD_DIM)
    freqs = ROPE_THETA ** (-(lane % half).astype(F32) / half)
    ang = pos.astype(F32)[:, None] * freqs[None, :]
    sign = jnp.where(lane < half, -1.0, 1.0).astype(F32)
    return jnp.cos(ang), jnp.sin(ang) * sign[None, :]


def _prep_ab(li, ab_norm_pre, ab_norm_post, ab_w_in, ab_w_out, mlstm_conv_w, mlstm_conv_b, mlstm_wq, mlstm_wk,
             mlstm_wv, mlstm_w_gates, mlstm_b_gates, mlstm_out_norm, mlstm_skip):
    ng = 2 * B_HEADS
    wg = jnp.pad(mlstm_w_gates[li], ((0, 0), (0, LANES - ng))).astype(BF16)
    wqk = jnp.concatenate([_blockdiag_dense(mlstm_wq[li]), _blockdiag_dense(mlstm_wk[li])], axis=2)
    wqk_hi, wqk_lo = _hi_lo(wqk)
    wv_hi, wv_lo = _hi_lo(_blockdiag_dense(mlstm_wv[li]))
    return dict(
        norm_pre=ab_norm_pre[li], norm_post=ab_norm_post[li].reshape(1, D_MODEL),
        w_in=ab_w_in[li].astype(BF16), w_out=ab_w_out[li].astype(BF16),
        conv_w=mlstm_conv_w[li], conv_b=mlstm_conv_b[li].reshape(1, B_WIDTH),
        wqk_hi=wqk_hi, wqk_lo=wqk_lo, wv_hi=wv_hi, wv_lo=wv_lo,
        wg=wg, wgt=mlstm_w_gates[li].T.astype(BF16),
        bg=jnp.pad(mlstm_b_gates[li], (0, LANES - ng)).reshape(1, LANES),
        bgt=mlstm_b_gates[li].reshape(ng, 1),
        out_norm=mlstm_out_norm[li].reshape(1, B_WIDTH), skip=mlstm_skip[li].reshape(1, B_WIDTH))


def _prep_c(li, c_norm_pre, c_norm_post, c_w_in, c_w_out, c_conv_w, c_conv_b, c_w_a, c_b_a, c_w_x, c_b_x,
            c_lambda):
    return dict(
        norm_pre=c_norm_pre[li], norm_post=c_norm_post[li].reshape(1, D_MODEL),
        w_in=c_w_in[li].astype(BF16), w_out=c_w_out[li].astype(BF16),
        conv_w=c_conv_w[li], conv_b=c_conv_b[li].reshape(1, C_WIDTH),
        w_a=c_w_a[li].astype(BF16), b_a=c_b_a[li].reshape(1, C_WIDTH),
        w_x=c_w_x[li].astype(BF16), b_x=c_b_x[li].reshape(1, C_WIDTH),
        lam=c_lambda[li].reshape(1, C_WIDTH))


def _prompt_trunk(x_prompt, ab_layers, c_layers):
    bsz, seq, _ = x_prompt.shape
    x = x_prompt.reshape(bsz * seq, D_MODEL)
    cos_t, sin_t = _rope_tables(jnp.arange(seq, dtype=jnp.int32))
    zeros = jnp.zeros((seq, LANES), F32)
    outs = [[] for _ in range(8)]
    keep = min(A_MAX_WINDOW, seq)
    for li in range(len(ab_layers)):
        lw = ab_layers[li]
        proj = _norm_inproj(x, lw["norm_pre"], lw["w_in"], cos_t, sin_t, n_rope=2, tm=TM_INPROJ)
        oa = _dilated_attn(proj, bsz, seq)
        x, c1, n1, m1, buf = _mlstm_prompt(x, proj, oa, lw, bsz, seq)
        pv = proj.reshape(bsz, seq, AB_IN)
        new_k = pv[:, seq - keep:, A_WIDTH:2 * A_WIDTH].reshape(bsz, keep, A_HEADS, A_HEAD_DIM)
        new_v = pv[:, seq - keep:, 2 * A_WIDTH:3 * A_WIDTH].reshape(bsz, keep, A_HEADS, A_HEAD_DIM)
        for j, o in enumerate((new_k, new_v, c1, n1, m1, buf)):
            outs[j].append(o)
        if li < len(c_layers):
            lc = c_layers[li]
            projc = _norm_inproj(x, lc["norm_pre"], lc["w_in"], zeros, zeros, n_rope=0, tm=TM_INPROJ)
            x, h_last, cbuf = _rglru_prompt(x, projc, lc, bsz, seq, TM_RGLRU)
            outs[6].append(h_last)
            outs[7].append(cbuf)
    return x.reshape(bsz, seq, D_MODEL), [jnp.stack(o, axis=0) for o in outs]


def _sample_trunk(x_sample, cache_k, cache_v, st, ab_layers, c_layers):
    bd = x_sample.shape[0]
    x = x_sample.reshape(bd, D_MODEL)
    cos_t, sin_t = _rope_tables(jnp.full((bd,), PAST_LEN, dtype=jnp.int32))
    zeros = jnp.zeros((bd, LANES), F32)
    outs = [[] for _ in range(8)]
    upds = []
    for li in range(len(ab_layers)):
        lw = ab_layers[li]
        proj = _norm_inproj(x, lw["norm_pre"], lw["w_in"], cos_t, sin_t, n_rope=2, tm=bd)
        qkv = proj[:, :3 * A_WIDTH].reshape(bd, 3 * A_HEADS, A_HEAD_DIM)
        oa = _decode_attn(qkv, cache_k, cache_v, li)
        dec, upd, n1, m1, buf = _mlstm_decode(proj, st[3][li], st[0], li, st[1][li], st[2][li], lw)
        upds.append(upd)
        new_k = qkv[:, A_HEADS:2 * A_HEADS].reshape(bd, 1, A_HEADS, A_HEAD_DIM)
        new_v = qkv[:, 2 * A_HEADS:].reshape(bd, 1, A_HEADS, A_HEAD_DIM)
        x = _ab_out_decode(x, proj, lw, oa.reshape(bd, A_WIDTH), dec)
        for j, o in zip((0, 1, 3, 4, 5), (new_k, new_v, n1, m1, buf)):
            outs[j].append(o)
        if li < len(c_layers):
            lc = c_layers[li]
            projc = _norm_inproj(x, lc["norm_pre"], lc["w_in"], zeros, zeros, n_rope=0, tm=bd)
            x, h, cbuf = _rglru_decode(x, projc, st[5][li], st[4][li], lc)
            outs[6].append(h)
            outs[7].append(cbuf)
    stacked = [None if j == 2 else jnp.stack(o, axis=0) for j, o in enumerate(outs)]
    stacked[2] = _mlstm_decode_cnew(st[0], upds)
    return x.reshape(bd, 1, D_MODEL), stacked


def kernel(x_prompt, x_sample, cache_k, cache_v, state_mlstm_c, state_mlstm_n, state_mlstm_m, state_mlstm_conv,
           state_rglru_h, state_rglru_conv, ab_norm_pre, ab_norm_post, ab_w_in, ab_w_out, mlstm_conv_w,
           mlstm_conv_b, mlstm_wq, mlstm_wk, mlstm_wv, mlstm_w_gates, mlstm_b_gates, mlstm_out_norm, mlstm_skip,
           c_norm_pre, c_norm_post, c_w_in, c_w_out, c_conv_w, c_conv_b, c_w_a, c_b_a, c_w_x, c_b_x, c_lambda):
    n_ab = ab_w_in.shape[0]
    n_c = c_w_in.shape[0]
    ab_layers = [_prep_ab(li, ab_norm_pre, ab_norm_post, ab_w_in, ab_w_out, mlstm_conv_w, mlstm_conv_b, mlstm_wq,
                          mlstm_wk, mlstm_wv, mlstm_w_gates, mlstm_b_gates, mlstm_out_norm, mlstm_skip)
                 for li in range(n_ab)]
    c_layers = [_prep_c(li, c_norm_pre, c_norm_post, c_w_in, c_w_out, c_conv_w, c_conv_b, c_w_a, c_b_a, c_w_x,
                        c_b_x, c_lambda) for li in range(n_c)]
    y_prompt, pst = _prompt_trunk(x_prompt, ab_layers, c_layers)
    st = (state_mlstm_c, state_mlstm_n, state_mlstm_m, state_mlstm_conv, state_rglru_h, state_rglru_conv)
    y_sample, sst = _sample_trunk(x_sample, cache_k, cache_v, st, ab_layers, c_layers)
    return (y_prompt, y_sample, *pst, *sst)
```

```python
import functools

import jax
import jax.numpy as jnp
from jax import lax
from jax.experimental import pallas as pl
from jax.experimental.pallas import tpu as pltpu

F32 = jnp.float32
BF16 = jnp.bfloat16

D_MODEL = 1024
PAST_LEN = 2048
A_HEADS = 8
A_HEAD_DIM = 128
A_WIDTH = 1024
A_PATTERNS = ((128, 1), (512, 4), (2048, 16))
A_MAX_WINDOW = 2048
ROPE_THETA = 10000.0
B_HEADS = 4
B_WIDTH = 1024
B_HEAD_DIM = 256
B_QKV_BLOCK = 4
B_CHUNK = 128
C_WIDTH = 1536
C_BLOCKS = 12
C_BLOCK = 128
RG_LRU_C = 8.0
CONV_WIDTH = 4
AB_IN = 6 * 1024
C_IN = 2 * C_WIDTH
NORM_EPS = 1e-6

LANES = 128
SUBLANES = 8
VMEM_LIMIT = 56 * 1024 * 1024

TM_INPROJ = 256
TM_RGLRU = 256
ATT_TQ = A_MAX_WINDOW
ATT_BLK = 128
ATT_QB = 128
NB_DECODE_ATTN = 4
NB_DECODE_STATE = 16


def _cparams(*sem):
    return pltpu.CompilerParams(dimension_semantics=sem, vmem_limit_bytes=VMEM_LIMIT)


def _sigmoid(x):
    return 0.5 * jnp.tanh(0.5 * x) + 0.5


def _causal_conv(x, xs_scr, cw_ref, cb_ref):
    t, w = x.shape
    halo = SUBLANES
    pitch = 2 * (t + halo)
    parts = []
    for g in range(w // LANES):
        ls = slice(g * LANES, (g + 1) * LANES)
        base = g * pitch
        xg = x[:, ls]
        xs_scr[pl.ds(base + 2 * halo, t, stride=2), :] = xg
        acc = xg * cw_ref[CONV_WIDTH - 1:CONV_WIDTH, ls] + cb_ref[:, ls]
        for s in range(1, CONV_WIDTH):
            acc = acc + xs_scr[pl.ds(base + 2 * (halo - s), t, stride=2), :] * cw_ref[CONV_WIDTH - 1 - s:CONV_WIDTH - s, ls]
        xs_scr[pl.ds(base, halo, stride=2), :] = xg[t - halo:t, :]
        parts.append(acc)
    return jnp.concatenate(parts, axis=1)


def _silu(x):
    return x * _sigmoid(x)


def _log_sigmoid(x):
    return jnp.minimum(x, 0.0) - jnp.log1p(jnp.exp(-jnp.abs(x)))


def _softplus(x):
    return jnp.maximum(x, 0.0) + jnp.log1p(jnp.exp(-jnp.abs(x)))


def _rmsnorm(x, g):
    return x * lax.rsqrt(jnp.mean(x * x, axis=-1, keepdims=True) + NORM_EPS) * g


def _dot(a, b):
    return jnp.dot(a, b, preferred_element_type=F32)


def _dot_nt(a, b):
    return lax.dot_general(a, b, (((1,), (1,)), ((), ())), preferred_element_type=F32)


def _dot_tn(a, b):
    return lax.dot_general(a, b, (((0,), (0,)), ((), ())), preferred_element_type=F32)


def _inproj_kernel(x_ref, g_ref, w_ref, cos_ref, sin_ref, o_ref, *, n_rope):
    h = _rmsnorm(x_ref[...], g_ref[...]).astype(BF16)
    group = A_WIDTH
    for j in range(w_ref.shape[1] // group):
        p = _dot(h, w_ref[:, j * group:(j + 1) * group])
        if j < n_rope:
            c = cos_ref[...]
            s = sin_ref[...]
            for hd in range(group // A_HEAD_DIM):
                ph = p[:, hd * A_HEAD_DIM:(hd + 1) * A_HEAD_DIM]
                lo = j * group + hd * A_HEAD_DIM
                o_ref[:, lo:lo + A_HEAD_DIM] = ph * c + pltpu.roll(ph, A_HEAD_DIM // 2, 1) * s
        else:
            o_ref[:, j * group:(j + 1) * group] = p


def _norm_inproj(x, g, w_bf16, cos_t, sin_t, *, n_rope, tm):
    t, d = x.shape
    n = w_bf16.shape[1]
    n_pos_tiles = cos_t.shape[0] // tm
    return pl.pallas_call(
        functools.partial(_inproj_kernel, n_rope=n_rope),
        grid=(t // tm,),
        in_specs=[
            pl.BlockSpec((tm, d), lambda i: (i, 0)),
            pl.BlockSpec((1, d), lambda i: (0, 0)),
            pl.BlockSpec((d, n), lambda i: (0, 0)),
            pl.BlockSpec((tm, LANES), lambda i: (i % n_pos_tiles, 0)),
            pl.BlockSpec((tm, LANES), lambda i: (i % n_pos_tiles, 0)),
        ],
        out_specs=pl.BlockSpec((tm, n), lambda i: (i, 0)),
        out_shape=jax.ShapeDtypeStruct((t, n), F32),
        compiler_params=_cparams("parallel"),
        name="norm_inproj",
    )(x, g.reshape(1, d), w_bf16, cos_t, sin_t)


def _attend(q, k, v, bias):
    sc = _dot_nt(q, k) * (A_HEAD_DIM ** -0.5) + bias
    m = jnp.max(sc, axis=-1, keepdims=True)
    p = jnp.exp(sc - m)
    l = jnp.sum(p, axis=-1, keepdims=True)
    o = _dot(p.astype(BF16), v) / l
    return o, jnp.broadcast_to(m + jnp.log(l), (q.shape[0], LANES))


def _attn_kernel(q_ref, k_ref, v_ref, o_ref, d4f, q1, q4, q16, k1e, k4e, k16e, v1e, v4e, v16e,
                 o4n, o16n, l4n, l16n):
    i = pl.program_id(2)
    blk = ATT_BLK
    tq = ATT_TQ
    n4 = tq // 4
    n16 = tq // 16

    @pl.when(i == 0)
    def _():
        for r in (k1e, v1e):
            r[0:blk, :] = jnp.zeros((blk, LANES), BF16)
        for r in (k4e, v4e, k16e, v16e):
            r[:, 0:blk, :] = jnp.zeros((r.shape[0], blk, LANES), BF16)

    @pl.when(i > 0)
    def _():
        for r in (k1e, v1e):
            r[0:blk, :] = r[tq:tq + blk, :]
        for r in (k4e, v4e):
            r[:, 0:blk, :] = r[:, n4:n4 + blk, :]
        for r in (k16e, v16e):
            r[:, 0:blk, :] = r[:, n16:n16 + blk, :]

    def split(src_ref, d1, d4, d16, off):
        def natural(c, carry):
            r0 = pl.multiple_of(c * 2 * blk, 2 * blk)
            d1[pl.ds(off + r0, 2 * blk), :] = src_ref[0, pl.ds(r0, 2 * blk), :].astype(BF16)
            return carry
        lax.fori_loop(0, tq // (2 * blk), natural, 0)
        for r4 in range(4):
            for c in range(n4 // blk):
                x = src_ref[0, pl.ds(r4 + 4 * c * blk, blk, stride=4), :]
                d4f[r4 * n4 + c * blk:r4 * n4 + (c + 1) * blk, :] = x
                d4[r4, off + c * blk:off + (c + 1) * blk, :] = x.astype(BF16)
        for r4 in range(4):
            for a in range(4):
                x = d4f[pl.ds(r4 * n4 + a, n16, stride=4), :]
                d16[r4 + 4 * a, off:off + n16, :] = x.astype(BF16)

    split(q_ref, q1, q4, q16, 0)
    split(k_ref, k1e, k4e, k16e, blk)
    split(v_ref, v1e, v4e, v16e, blk)

    qb = ATT_QB
    row = lax.broadcasted_iota(jnp.int32, (qb, qb + blk), 0)
    col = lax.broadcasted_iota(jnp.int32, (qb, qb + blk), 1)
    bias = jnp.where((col >= row) & (col - row <= blk), 0.0, -jnp.inf)
    bias_at = [jnp.where(col >= jnp.where(i == 0, blk - s * qb, 0), bias, -jnp.inf) for s in range(blk // qb)]
    bias_at += [bias] * (tq // qb)

    for r in range(16):
        for s in range(n16 // qb):
            win = slice(s * qb, (s + 1) * qb + blk)
            o, lse = _attend(q16[r, s * qb:(s + 1) * qb, :], k16e[r, win, :], v16e[r, win, :], bias_at[s])
            o16n[pl.ds(16 * s * qb + r, qb, stride=16), :] = o
            l16n[pl.ds(16 * s * qb + r, qb, stride=16), :] = lse

    for r4 in range(4):
        for s in range(n4 // qb):
            win = slice(s * qb, (s + 1) * qb + blk)
            o, lse = _attend(q4[r4, s * qb:(s + 1) * qb, :], k4e[r4, win, :], v4e[r4, win, :], bias_at[s])
            o4n[pl.ds(4 * s * qb + r4, qb, stride=4), :] = o
            l4n[pl.ds(4 * s * qb + r4, qb, stride=4), :] = lse

    for s in range(tq // qb):
        rows = slice(s * qb, (s + 1) * qb)
        win = slice(s * qb, (s + 1) * qb + blk)
        o1, l1 = _attend(q1[rows, :], k1e[win, :], v1e[win, :], bias_at[s])
        l4, l16 = l4n[rows, :], l16n[rows, :]
        mx = jnp.maximum(jnp.maximum(l1, l4), l16)
        e1, e4, e16 = jnp.exp(l1 - mx), jnp.exp(l4 - mx), jnp.exp(l16 - mx)
        o_ref[0, rows, :] = (o1 * e1 + o4n[rows, :] * e4 + o16n[rows, :] * e16) / (e1 + e4 + e16)


def _dilated_attn(proj, bsz, seq):
    tq, blk = ATT_TQ, ATT_BLK
    view = proj.reshape(bsz, seq, AB_IN)
    col = lambda c: pl.BlockSpec((1, tq, A_HEAD_DIM), lambda b, h, i: (b, i, c * A_HEADS + h))
    bf = lambda *s: pltpu.VMEM(s, BF16)
    nat = pltpu.VMEM((tq, LANES), F32)
    out = pl.pallas_call(
        _attn_kernel,
        grid=(bsz, A_HEADS, seq // tq),
        in_specs=[col(0), col(1), col(2)],
        out_specs=pl.BlockSpec((1, tq, A_HEAD_DIM), lambda b, h, i: (b, i, h)),
        out_shape=jax.ShapeDtypeStruct((bsz, seq, A_WIDTH), F32),
        scratch_shapes=[nat,
                        bf(tq, LANES), bf(4, tq // 4, LANES), bf(16, tq // 16, LANES),
                        bf(tq + blk, LANES), bf(4, tq // 4 + blk, LANES), bf(16, tq // 16 + blk, LANES),
                        bf(tq + blk, LANES), bf(4, tq // 4 + blk, LANES), bf(16, tq // 16 + blk, LANES),
                        nat, nat, nat, nat],
        compiler_params=_cparams("parallel", "parallel", "arbitrary"),
        name="dilated_attn",
    )(view, view, view)
    return out.reshape(bsz * seq, A_WIDTH)


def _decode_attn_kernel(q_ref, kn_ref, vn_ref, k1_ref, k4_ref, k16_ref, v1_ref, v4_ref, v16_ref, o_ref, *, nb):
    scale = A_HEAD_DIM ** -0.5
    for bb in range(nb):
        q = q_ref[bb]
        kn = kn_ref[bb]
        vn = vn_ref[bb]
        s_new = jnp.sum(q * kn, axis=-1, keepdims=True) * scale
        outs, lses = [], []
        for k_ref, v_ref in ((k1_ref, v1_ref), (k4_ref, v4_ref), (k16_ref, v16_ref)):
            kk = k_ref[0, bb, :, 0]
            vv = v_ref[0, bb, :, 0]
            sc = jnp.sum(kk * q[None], axis=-1, keepdims=True) * scale
            m = jnp.maximum(jnp.max(sc, axis=0), s_new)
            p = jnp.exp(sc - m[None])
            p_new = jnp.exp(s_new - m)
            l = jnp.sum(p, axis=0) + p_new
            outs.append((jnp.sum(p * vv, axis=0) + p_new * vn) / l)
            lses.append(m + jnp.log(l))
        mx = jnp.maximum(jnp.maximum(lses[0], lses[1]), lses[2])
        e = [jnp.exp(x - mx) for x in lses]
        tot = e[0] + e[1] + e[2]
        o_ref[bb] = (outs[0] * e[0] + outs[1] * e[1] + outs[2] * e[2]) / tot


def _decode_attn(qkv, cache_k, cache_v, li):
    bd = qkv.shape[0]
    n_layers, _, win = cache_k.shape[:3]
    nb = NB_DECODE_ATTN
    new = lambda c: pl.BlockSpec((nb, A_HEADS, A_HEAD_DIM), lambda i: (i, c, 0))
    specs, views = [], []
    for cache in (cache_k, cache_v):
        for window, dil in A_PATTERNS:
            rows = win // dil
            n_keys = window // dil
            views.append(cache.reshape(n_layers, bd, rows, dil, A_HEADS, A_HEAD_DIM))
            specs.append(pl.BlockSpec((1, nb, n_keys, 1, A_HEADS, A_HEAD_DIM),
                                      lambda i, rb=rows // n_keys - 1: (li, i, rb, 0, 0, 0)))
    return pl.pallas_call(
        functools.partial(_decode_attn_kernel, nb=nb),
        grid=(bd // nb,),
        in_specs=[new(0), new(1), new(2)] + specs,
        out_specs=pl.BlockSpec((nb, A_HEADS, A_HEAD_DIM), lambda i: (i, 0, 0)),
        out_shape=jax.ShapeDtypeStruct((bd, A_HEADS, A_HEAD_DIM), F32),
        compiler_params=_cparams("parallel"),
        name="decode_attn",
    )(qkv, qkv, qkv, *views)


def _blockdiag_dense(w):
    nb = w.shape[0]
    per = LANES // B_QKV_BLOCK
    w4 = w.reshape(nb // per, per, B_QKV_BLOCK, B_QKV_BLOCK)
    return jnp.einsum('gnio,nm->gnimo', w4, jnp.eye(per, dtype=w.dtype)).reshape(nb // per, LANES, LANES)


def _hi_lo(x):
    hi = x.astype(BF16)
    return hi, (x - hi.astype(F32)).astype(BF16)


def _blockdiag_mxu(x, whi_ref, wlo_ref):
    outs = []
    for g in range(x.shape[1] // LANES):
        hi, lo = _hi_lo(x[:, g * LANES:(g + 1) * LANES])
        outs.append(_dot(hi, whi_ref[g]) + _dot(lo, whi_ref[g]) + _dot(hi, wlo_ref[g]))
    return outs


def _mlstm_qkv(xb, xc, wqk_hi, wqk_lo, wv_hi, wv_lo):
    qk = _blockdiag_mxu(xc, wqk_hi, wqk_lo)
    qb = jnp.concatenate([o[:, :LANES] for o in qk], axis=1)
    kb = jnp.concatenate([o[:, LANES:] for o in qk], axis=1)
    vb = jnp.concatenate(_blockdiag_mxu(xb, wv_hi, wv_lo), axis=1)
    return qb, kb, vb


def _head_out(hb_raw, xc, on_ref, sk_ref):
    parts = []
    for h in range(B_HEADS):
        xh = hb_raw[:, h * B_HEAD_DIM:(h + 1) * B_HEAD_DIM]
        xcn = xh - jnp.mean(xh, axis=-1, keepdims=True)
        parts.append(xcn * lax.rsqrt(jnp.mean(xcn * xcn, axis=-1, keepdims=True) + 1e-5))
    return jnp.concatenate(parts, axis=1) * on_ref[...] + sk_ref[...] * xc


def _cumsum_lanes(x):
    t = x.shape[1]
    idx = lax.broadcasted_iota(jnp.int32, x.shape, 1)
    k = 1
    while k < t:
        x = x + jnp.where(idx >= k, pltpu.roll(x, k, 1), 0.0)
        k *= 2
    return x


def _mlstm_chunk(x, xx_scr, hraw_scr, c_scr, n_scr, m_scr, cw_ref, cb_ref, wqk_hi, wqk_lo, wv_hi, wv_lo,
                 wgt_ref, bgt_ref, on_ref, sk_ref):
    t = B_CHUNK
    ng = 2 * B_HEADS
    xc = _silu(_causal_conv(x, xx_scr, cw_ref, cb_ref))

    qb, kb, vb = _mlstm_qkv(x, xc, wqk_hi, wqk_lo, wv_hi, wv_lo)
    qh, kh, vh = qb.astype(BF16), kb.astype(BF16), vb.astype(BF16)
    gt = (_dot_nt(wgt_ref[:, 0:B_WIDTH], qh) + _dot_nt(wgt_ref[:, B_WIDTH:2 * B_WIDTH], kh)
          + _dot_nt(wgt_ref[:, 2 * B_WIDTH:3 * B_WIDTH], vh) + bgt_ref[...])
    brow = _cumsum_lanes(_log_sigmoid(gt))
    gcol = jnp.concatenate([gt, brow, jnp.zeros((t - 2 * ng, t), F32)], axis=0).T
    ri = lax.broadcasted_iota(jnp.int32, (t, t), 0)
    cj = lax.broadcasted_iota(jnp.int32, (t, t), 1)
    causal = cj <= ri
    kscale = B_HEAD_DIM ** -0.5
    for h in range(B_HEADS):
        sl = slice(h * B_HEAD_DIM, (h + 1) * B_HEAD_DIM)
        q = qb[:, sl]
        k = kb[:, sl] * kscale
        qhh = qh[:, sl]
        khh = k.astype(BF16)
        vhh = vh[:, sl]
        logi_c = gcol[:, h:h + 1]
        b_c = gcol[:, ng + B_HEADS + h:ng + B_HEADS + h + 1]
        logi_r = gt[h:h + 1, :]
        b_r = brow[B_HEADS + h:B_HEADS + h + 1, :]
        m0 = m_scr[h:h + 1, 0:1]
        n0 = n_scr[h:h + 1, :]
        c0 = c_scr[h]
        dmat = jnp.where(causal, b_c - b_r + logi_r, -jnp.inf)
        inter = b_c + m0
        m = jnp.maximum(jnp.max(dmat, axis=1, keepdims=True), inter)
        w_intra = jnp.exp(dmat - m)
        w_inter = jnp.exp(inter - m)
        s = _dot_nt(qhh, khh) * w_intra
        num = _dot(s.astype(BF16), vhh) + w_inter * _dot(qhh, c0.astype(BF16))
        den = jnp.sum(s, axis=1, keepdims=True) + w_inter * jnp.sum(q * n0, axis=1, keepdims=True)
        hraw_scr[:, sl] = num / jnp.maximum(jnp.abs(den), jnp.exp(-m))
        bl = b_c[t - 1:t, :]
        dl = bl - b_c + logi_c
        m_new = jnp.maximum(bl + m0, jnp.max(dl, axis=0, keepdims=True))
        kw = jnp.exp(dl - m_new) * k
        carry = jnp.exp(bl + m0 - m_new)
        c_scr[h] = carry * c0 + _dot_tn(kw.astype(BF16), vhh)
        n_scr[h:h + 1, :] = carry * n0 + jnp.sum(kw, axis=0, keepdims=True)
        m_scr[h:h + 1, :] = jnp.broadcast_to(m_new, (1, LANES))
    return _head_out(hraw_scr[...], xc, on_ref, sk_ref)


def _gate_project(oa, hb, za, zb, x, w_ref, g_ref):
    ya = oa * _silu(za)
    yb = hb * _silu(zb)
    y = _dot(ya.astype(BF16), w_ref[0:A_WIDTH, :]) + _dot(yb.astype(BF16), w_ref[A_WIDTH:A_WIDTH + B_WIDTH, :])
    return x + _rmsnorm(y, g_ref[...])


def _mlstm_prompt_kernel(xb_ref, oa_ref, za_ref, zb_ref, x_ref, cw_ref, cb_ref, wqk_hi, wqk_lo, wv_hi, wv_lo,
                         wgt_ref, bgt_ref, on_ref, sk_ref, w_ref, g_ref,
                         o_ref, c_out, n_out, m_out, buf_out,
                         xx_scr, hraw_scr, c_scr, n_scr, m_scr):
    ci = pl.program_id(0)
    nc = pl.num_programs(0)
    t = B_CHUNK
    halo = SUBLANES
    bsz = xb_ref.shape[0]

    @pl.when(ci == 0)
    def _():
        xx_scr[...] = jnp.zeros_like(xx_scr)
        c_scr[...] = jnp.zeros_like(c_scr)
        n_scr[...] = jnp.zeros_like(n_scr)
        m_scr[...] = jnp.zeros_like(m_scr)

    hb = [_mlstm_chunk(xb_ref[b], xx_scr.at[b], hraw_scr.at[b], c_scr.at[b], n_scr.at[b], m_scr.at[b],
                       cw_ref, cb_ref, wqk_hi, wqk_lo, wv_hi, wv_lo, wgt_ref, bgt_ref, on_ref, sk_ref)
          for b in range(bsz)]
    rows = lambda r: r[...].reshape(bsz * t, r.shape[-1])
    y = _gate_project(rows(oa_ref), jnp.concatenate(hb, axis=0), rows(za_ref), rows(zb_ref), rows(x_ref),
                      w_ref, g_ref)
    o_ref[...] = y.reshape(bsz, t, D_MODEL)

    @pl.when(ci == nc - 1)
    def _():
        c_out[...] = c_scr[...]
        n_out[...] = n_scr[:, 0:B_HEADS, :]
        lane = lax.broadcasted_iota(jnp.int32, (1, LANES), 1)
        for b in range(bsz):
            mt = jnp.zeros((1, LANES), F32)
            for h in range(B_HEADS):
                mt = jnp.where(lane == h, m_scr[b, h:h + 1, :], mt)
            m_out[b] = mt
            buf_out[b] = xb_ref[b, t - halo:t, :]


def _mlstm_prompt(x, proj, oa, lw, bsz, seq):
    t = B_CHUNK
    view = proj.reshape(bsz, seq, AB_IN)
    full = lambda a: pl.BlockSpec(a.shape, lambda c: (0,) * a.ndim)
    pcol = lambda col: pl.BlockSpec((bsz, t, B_WIDTH), lambda c: (0, c, col))
    row = pl.BlockSpec((bsz, t, D_MODEL), lambda c: (0, c, 0))
    whole = lambda *s: pl.BlockSpec(s, lambda c: (0,) * len(s))
    params = [lw["conv_w"], lw["conv_b"], lw["wqk_hi"], lw["wqk_lo"], lw["wv_hi"], lw["wv_lo"], lw["wgt"],
              lw["bgt"], lw["out_norm"], lw["skip"], lw["w_out"], lw["norm_post"]]
    sds = jax.ShapeDtypeStruct
    y, c, n, m, buf = pl.pallas_call(
        _mlstm_prompt_kernel,
        grid=(seq // t,),
        in_specs=[pcol(4), row, pcol(3), pcol(5), row] + [full(a) for a in params],
        out_specs=[row,
                   whole(bsz, B_HEADS, B_HEAD_DIM, B_HEAD_DIM),
                   whole(bsz, B_HEADS, B_HEAD_DIM),
                   whole(bsz, 1, LANES),
                   whole(bsz, SUBLANES, B_WIDTH)],
        out_shape=[sds((bsz, seq, D_MODEL), F32),
                   sds((bsz, B_HEADS, B_HEAD_DIM, B_HEAD_DIM), F32),
                   sds((bsz, B_HEADS, B_HEAD_DIM), F32),
                   sds((bsz, 1, LANES), F32),
                   sds((bsz, SUBLANES, B_WIDTH), F32)],
        scratch_shapes=[pltpu.VMEM((bsz, (B_WIDTH // LANES) * 2 * (t + SUBLANES), LANES), F32),
                        pltpu.VMEM((bsz, t, B_WIDTH), F32),
                        pltpu.VMEM((bsz, B_HEADS, B_HEAD_DIM, B_HEAD_DIM), F32),
                        pltpu.VMEM((bsz, SUBLANES, B_HEAD_DIM), F32),
                        pltpu.VMEM((bsz, SUBLANES, LANES), F32)],
        compiler_params=_cparams("arbitrary"),
        name="mlstm_prompt",
    )(view, oa.reshape(bsz, seq, A_WIDTH), view, view, x.reshape(bsz, seq, D_MODEL), *params)
    return (y.reshape(bsz * seq, D_MODEL), c, n, m[:, 0, :B_HEADS], buf[:, SUBLANES - (CONV_WIDTH - 1):, :])


def _mlstm_decode_pre_kernel(xb_ref, b0_ref, b1_ref, b2_ref, cw_ref, cb_ref, wqk_hi, wqk_lo, wv_hi, wv_lo, wg_ref,
                             bg_ref, n0_ref, m0_ref,
                             xc_ref, v_ref, sv_ref, wi_ref, den_ref, n_out, m_out, buf_out, qt_ref, kwt_ref):
    x = xb_ref[...]
    xconv = (x * cw_ref[3:4, :] + b2_ref[...] * cw_ref[2:3, :] + b1_ref[...] * cw_ref[1:2, :]
             + b0_ref[...] * cw_ref[0:1, :] + cb_ref[...])
    xc = _silu(xconv)
    xc_ref[...] = xc
    buf_out[:, 0:B_WIDTH] = b1_ref[...]
    buf_out[:, B_WIDTH:2 * B_WIDTH] = b2_ref[...]
    buf_out[:, 2 * B_WIDTH:3 * B_WIDTH] = x
    qb, kb, vb = _mlstm_qkv(x, xc, wqk_hi, wqk_lo, wv_hi, wv_lo)
    g = (_dot(qb.astype(BF16), wg_ref[0:B_WIDTH, :]) + _dot(kb.astype(BF16), wg_ref[B_WIDTH:2 * B_WIDTH, :])
         + _dot(vb.astype(BF16), wg_ref[2 * B_WIDTH:3 * B_WIDTH, :]) + bg_ref[...])
    v_ref[...] = vb
    logf = _log_sigmoid(g)
    rows = x.shape[0]
    lane = lax.broadcasted_iota(jnp.int32, (rows, LANES), 1)
    m_tile = jnp.zeros((rows, LANES), F32)
    kscale = B_HEAD_DIM ** -0.5
    for h in range(B_HEADS):
        sl = slice(h * B_HEAD_DIM, (h + 1) * B_HEAD_DIM)
        logi = g[:, h:h + 1]
        fm = logf[:, B_HEADS + h:B_HEADS + h + 1] + m0_ref[:, h:h + 1]
        m = jnp.maximum(logi, fm)
        w_intra = jnp.exp(logi - m)
        w_inter = jnp.exp(fm - m)
        q = qb[:, sl]
        k = kb[:, sl] * kscale
        n0 = n0_ref[:, sl]
        s = jnp.sum(q * k, axis=1, keepdims=True) * w_intra
        den = s + w_inter * jnp.sum(q * n0, axis=1, keepdims=True)
        kw = w_intra * k
        sv_ref[:, sl] = s * vb[:, sl]
        wi_ref[:, sl] = jnp.broadcast_to(w_inter, q.shape)
        den_ref[:, sl] = jnp.broadcast_to(jnp.maximum(jnp.abs(den), jnp.exp(-m)), q.shape)
        n_out[:, sl] = w_inter * n0 + kw
        m_tile = jnp.where(lane == h, m, m_tile)
        qt_ref[h] = q.T
        kwt_ref[h] = kw.T
    m_out[...] = m_tile


def _column(mat_t, b):
    lane = lax.broadcasted_iota(jnp.int32, mat_t.shape, 1)
    return jnp.sum(jnp.where(lane == b, mat_t, 0.0), axis=1, keepdims=True)


def _mlstm_decode_qc_kernel(c0_ref, qt_ref, qc_ref, *, nb):
    i = pl.program_id(1)
    qt = qt_ref[0]
    for bb in range(nb):
        qc_ref[bb] = jnp.sum(c0_ref[0, bb, 0] * _column(qt, i * nb + bb), axis=0, keepdims=True)


def _mlstm_decode_cnew_kernel(c0_ref, kwt_ref, v_ref, wi_ref, c_out, *, nb):
    i = pl.program_id(2)
    kwt = kwt_ref[0, 0]
    for bb in range(nb):
        c_out[0, bb, 0] = wi_ref[0, bb] * c0_ref[0, bb, 0] + _column(kwt, i * nb + bb) * v_ref[0, bb]


def _mlstm_decode(proj, conv_state, state_c, li, n0, m0, lw):
    bd = proj.shape[0]
    w = B_WIDTH
    cs = conv_state.reshape(bd, (CONV_WIDTH - 1) * w)
    m0p = jnp.pad(m0, ((0, 0), (0, LANES - B_HEADS)))
    row = lambda c: pl.BlockSpec((bd, w), lambda i: (0, c))
    full = lambda a: pl.BlockSpec(a.shape, lambda i: (0,) * a.ndim)
    params = [lw["conv_w"], lw["conv_b"], lw["wqk_hi"], lw["wqk_lo"], lw["wv_hi"], lw["wv_lo"], lw["wg"], lw["bg"]]
    n0f = n0.reshape(bd, w)
    sds = jax.ShapeDtypeStruct
    tr = pl.BlockSpec((B_HEADS, B_HEAD_DIM, bd), lambda i: (0, 0, 0))
    outs = pl.pallas_call(
        _mlstm_decode_pre_kernel,
        grid=(1,),
        in_specs=[row(4), row(0), row(1), row(2)] + [full(a) for a in params] + [full(n0f), full(m0p)],
        out_specs=[row(0)] * 6 + [pl.BlockSpec((bd, LANES), lambda i: (0, 0)),
                                  pl.BlockSpec((bd, 3 * w), lambda i: (0, 0)), tr, tr],
        out_shape=[sds((bd, w), F32)] * 6 + [sds((bd, LANES), F32), sds((bd, 3 * w), F32),
                                             sds((B_HEADS, B_HEAD_DIM, bd), F32),
                                             sds((B_HEADS, B_HEAD_DIM, bd), F32)],
        compiler_params=_cparams("arbitrary"),
        name="mlstm_decode_pre",
    )(proj, cs, cs, cs, *params, n0f, m0p)
    xc, v, sv, wi, den, n_new, m_new, buf, qt, kwt = outs
    nb = NB_DECODE_STATE
    qc = pl.pallas_call(
        functools.partial(_mlstm_decode_qc_kernel, nb=nb),
        grid=(B_HEADS, bd // nb),
        in_specs=[pl.BlockSpec((1, nb, 1, B_HEAD_DIM, B_HEAD_DIM), lambda h, i: (li, i, h, 0, 0)),
                  pl.BlockSpec((1, B_HEAD_DIM, bd), lambda h, i: (h, 0, 0))],
        out_specs=pl.BlockSpec((nb, 1, B_HEAD_DIM), lambda h, i: (i, 0, h)),
        out_shape=sds((bd, 1, w), F32),
        compiler_params=_cparams("parallel", "arbitrary"),
        name="mlstm_decode_qc",
    )(state_c, qt)
    dec = dict(xc=xc, sv=sv, wi=wi, den=den, qc=qc.reshape(bd, w))
    upd = dict(kwt=kwt, v=v.reshape(bd, 1, w), wi=wi.reshape(bd, 1, w))
    return (dec, upd, n_new.reshape(bd, B_HEADS, B_HEAD_DIM), m_new[:, :B_HEADS],
            buf.reshape(bd, CONV_WIDTH - 1, w))


def _mlstm_decode_cnew(state_c, upds):
    n_layers, bd = state_c.shape[:2]
    nb = NB_DECODE_STATE
    kwt = jnp.stack([u["kwt"] for u in upds], axis=0)
    v = jnp.stack([u["v"] for u in upds], axis=0)
    wi = jnp.stack([u["wi"] for u in upds], axis=0)
    cblk = pl.BlockSpec((1, nb, 1, B_HEAD_DIM, B_HEAD_DIM), lambda l, h, i: (l, i, h, 0, 0))
    rowh = pl.BlockSpec((1, nb, 1, B_HEAD_DIM), lambda l, h, i: (l, i, 0, h))
    return pl.pallas_call(
        functools.partial(_mlstm_decode_cnew_kernel, nb=nb),
        grid=(n_layers, B_HEADS, bd // nb),
        in_specs=[cblk, pl.BlockSpec((1, 1, B_HEAD_DIM, bd), lambda l, h, i: (l, h, 0, 0)), rowh, rowh],
        out_specs=cblk,
        out_shape=jax.ShapeDtypeStruct(state_c.shape, F32),
        compiler_params=_cparams("parallel", "parallel", "arbitrary"),
        name="mlstm_decode_cnew",
    )(state_c, kwt, v, wi)


def _ab_out_decode_kernel(oa_ref, sv_ref, wi_ref, qc_ref, den_ref, xc_ref, za_ref, zb_ref, x_ref,
                          w_ref, on_ref, sk_ref, g_ref, o_ref):
    hb = _head_out((sv_ref[...] + wi_ref[...] * qc_ref[...]) / den_ref[...], xc_ref[...], on_ref, sk_ref)
    o_ref[...] = _gate_project(oa_ref[...], hb, za_ref[...], zb_ref[...], x_ref[...], w_ref, g_ref)


def _ab_out_decode(x, proj, lw, oa, dec):
    bd, w = x.shape
    row = pl.BlockSpec((bd, w), lambda i: (0, 0))
    pcol = lambda c: pl.BlockSpec((bd, w), lambda i: (0, c))
    full = lambda a: pl.BlockSpec(a.shape, lambda i: (0,) * a.ndim)
    params = [lw["w_out"], lw["out_norm"], lw["skip"], lw["norm_post"]]
    return pl.pallas_call(
        _ab_out_decode_kernel,
        grid=(1,),
        in_specs=[row] * 6 + [pcol(3), pcol(5), row] + [full(a) for a in params],
        out_specs=row,
        out_shape=jax.ShapeDtypeStruct((bd, w), F32),
        compiler_params=_cparams("arbitrary"),
        name="ab_out_decode",
    )(oa, dec["sv"], dec["wi"], dec["qc"], dec["den"], dec["xc"], proj, proj, x, *params)


def _rglru_gates(xconv, wa_ref, ba_ref, wx_ref, bx_ref, lam_ref):
    rs, is_ = [], []
    for n in range(C_BLOCKS):
        sl = slice(n * C_BLOCK, (n + 1) * C_BLOCK)
        xa = xconv[:, sl].astype(BF16)
        rs.append(_dot(xa, wa_ref[n]))
        is_.append(_dot(xa, wx_ref[n]))
    r = _sigmoid(jnp.concatenate(rs, axis=1) + ba_ref[...])
    ig = _sigmoid(jnp.concatenate(is_, axis=1) + bx_ref[...])
    log_a = -RG_LRU_C * r * _softplus(-lam_ref[...])
    a = jnp.exp(log_a)
    u = jnp.sqrt(1.0 - a * a) * (ig * xconv)
    return a, u


def _rglru_out(h, z, x, w_ref, g_ref):
    y = _dot((h * _silu(z)).astype(BF16), w_ref[...])
    return x + _rmsnorm(y, g_ref[...])


def _rglru_prompt_kernel(xc_ref, zc_ref, x_ref, cw_ref, cb_ref, wa_ref, ba_ref, wx_ref, bx_ref, lam_ref,
                         w_ref, g_ref, o_ref, h_out, buf_out, xx_scr, a_scr, u_scr, hc_scr, *, tm):
    ti = pl.program_id(1)
    nt = pl.num_programs(1)
    halo = SUBLANES

    @pl.when(ti == 0)
    def _():
        xx_scr[...] = jnp.zeros_like(xx_scr)
        hc_scr[...] = jnp.zeros_like(hc_scr)

    x = xc_ref[0]
    xconv = _causal_conv(x, xx_scr, cw_ref, cb_ref)
    a, u = _rglru_gates(xconv, wa_ref, ba_ref, wx_ref, bx_ref, lam_ref)
    a_scr[...] = a
    u_scr[...] = u
    ridx = lax.broadcasted_iota(jnp.int32, (SUBLANES, C_WIDTH), 0)

    def body(r, hprev):
        r8 = pl.multiple_of(r * SUBLANES, SUBLANES)
        aa = a_scr[pl.ds(r8, SUBLANES), :]
        uu = u_scr[pl.ds(r8, SUBLANES), :]
        for sft in (1, 2, 4):
            keep = ridx >= sft
            a_sh = jnp.where(keep, pltpu.roll(aa, sft, 0), 1.0)
            u_sh = jnp.where(keep, pltpu.roll(uu, sft, 0), 0.0)
            uu = aa * u_sh + uu
            aa = aa * a_sh
        hblk = aa * hprev + uu
        u_scr[pl.ds(r8, SUBLANES), :] = hblk
        return hblk[SUBLANES - 1:SUBLANES, :]

    hlast = lax.fori_loop(0, tm // SUBLANES, body, hc_scr[0:1, :])
    hc_scr[0:1, :] = hlast
    o_ref[0] = _rglru_out(u_scr[...], zc_ref[0], x_ref[0], w_ref, g_ref)

    @pl.when(ti == nt - 1)
    def _():
        h_out[0] = hlast
        buf_out[0] = x[tm - halo:tm, :]


def _rglru_decode_kernel(xc_ref, zc_ref, x_ref, b0_ref, b1_ref, b2_ref, h0_ref, cw_ref, cb_ref, wa_ref, ba_ref,
                         wx_ref, bx_ref, lam_ref, w_ref, g_ref, o_ref, h_out, buf_out):
    x = xc_ref[...]
    xconv = (x * cw_ref[3:4, :] + b2_ref[...] * cw_ref[2:3, :] + b1_ref[...] * cw_ref[1:2, :]
             + b0_ref[...] * cw_ref[0:1, :] + cb_ref[...])
    a, u = _rglru_gates(xconv, wa_ref, ba_ref, wx_ref, bx_ref, lam_ref)
    h = a * h0_ref[...] + u
    h_out[...] = h
    buf_out[:, 0:C_WIDTH] = b1_ref[...]
    buf_out[:, C_WIDTH:2 * C_WIDTH] = b2_ref[...]
    buf_out[:, 2 * C_WIDTH:3 * C_WIDTH] = x
    o_ref[...] = _rglru_out(h, zc_ref[...], x_ref[...], w_ref, g_ref)


def _c_params(lw):
    return [lw["conv_w"], lw["conv_b"], lw["w_a"], lw["b_a"], lw["w_x"], lw["b_x"], lw["lam"],
            lw["w_out"], lw["norm_post"]]


def _rglru_prompt(x, proj, lw, bsz, seq, tm):
    cw = C_WIDTH
    pv = proj.reshape(bsz, seq, C_IN)
    xv = x.reshape(bsz, seq, D_MODEL)
    params = _c_params(lw)
    full = lambda a: pl.BlockSpec(a.shape, lambda b, i: (0,) * a.ndim)
    sds = jax.ShapeDtypeStruct
    y, h_last, buf = pl.pallas_call(
        functools.partial(_rglru_prompt_kernel, tm=tm),
        grid=(bsz, seq // tm),
        in_specs=[pl.BlockSpec((1, tm, cw), lambda b, i: (b, i, 0)),
                  pl.BlockSpec((1, tm, cw), lambda b, i: (b, i, 1)),
                  pl.BlockSpec((1, tm, D_MODEL), lambda b, i: (b, i, 0))] + [full(a) for a in params],
        out_specs=[pl.BlockSpec((1, tm, D_MODEL), lambda b, i: (b, i, 0)),
                   pl.BlockSpec((1, 1, cw), lambda b, i: (b, 0, 0)),
                   pl.BlockSpec((1, SUBLANES, cw), lambda b, i: (b, 0, 0))],
        out_shape=[sds((bsz, seq, D_MODEL), F32), sds((bsz, 1, cw), F32), sds((bsz, SUBLANES, cw), F32)],
        scratch_shapes=[pltpu.VMEM(((cw // LANES) * 2 * (tm + SUBLANES), LANES), F32), pltpu.VMEM((tm, cw), F32),
                        pltpu.VMEM((tm, cw), F32), pltpu.VMEM((SUBLANES, cw), F32)],
        compiler_params=_cparams("parallel", "arbitrary"),
        name="rglru_prompt",
    )(pv, pv, xv, *params)
    return y.reshape(bsz * seq, D_MODEL), h_last[:, 0, :], buf[:, SUBLANES - (CONV_WIDTH - 1):, :]


def _rglru_decode(x, proj, conv_state, h0, lw):
    bd = x.shape[0]
    cw = C_WIDTH
    cs = conv_state.reshape(bd, (CONV_WIDTH - 1) * cw)
    params = _c_params(lw)
    col = lambda c: pl.BlockSpec((bd, cw), lambda i: (0, c))
    full = lambda a: pl.BlockSpec(a.shape, lambda i: (0,) * a.ndim)
    sds = jax.ShapeDtypeStruct
    y, h, buf = pl.pallas_call(
        _rglru_decode_kernel,
        grid=(1,),
        in_specs=[col(0), col(1), full(x), col(0), col(1), col(2), full(h0)] + [full(a) for a in params],
        out_specs=[pl.BlockSpec((bd, D_MODEL), lambda i: (0, 0)), pl.BlockSpec((bd, cw), lambda i: (0, 0)),
                   pl.BlockSpec((bd, 3 * cw), lambda i: (0, 0))],
        out_shape=[sds((bd, D_MODEL), F32), sds((bd, cw), F32), sds((bd, 3 * cw), F32)],
        compiler_params=_cparams("arbitrary"),
        name="rglru_decode",
    )(proj, proj, x, cs, cs, cs, h0, *params)
    return y, h, buf.reshape(bd, CONV_WIDTH - 1, cw)


def _rope_tables(pos):
    half = A_HEAD_DIM // 2
    lane = jnp.arange(A_HEAD_DIM)
    freqs = ROPE_THETA ** (-(lane % half).astype(F32) / half)
    ang = pos.astype(F32)[:, None] * freqs[None, :]
    sign = jnp.where(lane < half, -1.0, 1.0).astype(F32)
    return jnp.cos(ang), jnp.sin(ang) * sign[None, :]


def _prep_ab(li, ab_norm_pre, ab_norm_post, ab_w_in, ab_w_out, mlstm_conv_w, mlstm_conv_b, mlstm_wq, mlstm_wk,
             mlstm_wv, mlstm_w_gates, mlstm_b_gates, mlstm_out_norm, mlstm_skip):
    ng = 2 * B_HEADS
    wg = jnp.pad(mlstm_w_gates[li], ((0, 0), (0, LANES - ng))).astype(BF16)
    wqk = jnp.concatenate([_blockdiag_dense(mlstm_wq[li]), _blockdiag_dense(mlstm_wk[li])], axis=2)
    wqk_hi, wqk_lo = _hi_lo(wqk)
    wv_hi, wv_lo = _hi_lo(_blockdiag_dense(mlstm_wv[li]))
    return dict(
        norm_pre=ab_norm_pre[li], norm_post=ab_norm_post[li].reshape(1, D_MODEL),
        w_in=ab_w_in[li].astype(BF16), w_out=ab_w_out[li].astype(BF16),
        conv_w=mlstm_conv_w[li], conv_b=mlstm_conv_b[li].reshape(1, B_WIDTH),
        wqk_hi=wqk_hi, wqk_lo=wqk_lo, wv_hi=wv_hi, wv_lo=wv_lo,
        wg=wg, wgt=mlstm_w_gates[li].T.astype(BF16),
        bg=jnp.pad(mlstm_b_gates[li], (0, LANES - ng)).reshape(1, LANES),
        bgt=mlstm_b_gates[li].reshape(ng, 1),
        out_norm=mlstm_out_norm[li].reshape(1, B_WIDTH), skip=mlstm_skip[li].reshape(1, B_WIDTH))


def _prep_c(li, c_norm_pre, c_norm_post, c_w_in, c_w_out, c_conv_w, c_conv_b, c_w_a, c_b_a, c_w_x, c_b_x,
            c_lambda):
    return dict(
        norm_pre=c_norm_pre[li], norm_post=c_norm_post[li].reshape(1, D_MODEL),
        w_in=c_w_in[li].astype(BF16), w_out=c_w_out[li].astype(BF16),
        conv_w=c_conv_w[li], conv_b=c_conv_b[li].reshape(1, C_WIDTH),
        w_a=c_w_a[li].astype(BF16), b_a=c_b_a[li].reshape(1, C_WIDTH),
        w_x=c_w_x[li].astype(BF16), b_x=c_b_x[li].reshape(1, C_WIDTH),
        lam=c_lambda[li].reshape(1, C_WIDTH))


def _prompt_trunk(x_prompt, ab_layers, c_layers):
    bsz, seq, _ = x_prompt.shape
    x = x_prompt.reshape(bsz * seq, D_MODEL)
    cos_t, sin_t = _rope_tables(jnp.arange(seq, dtype=jnp.int32))
    zeros = jnp.zeros((seq, LANES), F32)
    outs = [[] for _ in range(8)]
    keep = min(A_MAX_WINDOW, seq)
    for li in range(len(ab_layers)):
        lw = ab_layers[li]
        proj = _norm_inproj(x, lw["norm_pre"], lw["w_in"], cos_t, sin_t, n_rope=2, tm=TM_INPROJ)
        oa = _dilated_attn(proj, bsz, seq)
        x, c1, n1, m1, buf = _mlstm_prompt(x, proj, oa, lw, bsz, seq)
        pv = proj.reshape(bsz, seq, AB_IN)
        new_k = pv[:, seq - keep:, A_WIDTH:2 * A_WIDTH].reshape(bsz, keep, A_HEADS, A_HEAD_DIM)
        new_v = pv[:, seq - keep:, 2 * A_WIDTH:3 * A_WIDTH].reshape(bsz, keep, A_HEADS, A_HEAD_DIM)
        for j, o in enumerate((new_k, new_v, c1, n1, m1, buf)):
            outs[j].append(o)
        if li < len(c_layers):
            lc = c_layers[li]
            projc = _norm_inproj(x, lc["norm_pre"], lc["w_in"], zeros, zeros, n_rope=0, tm=TM_INPROJ)
            x, h_last, cbuf = _rglru_prompt(x, projc, lc, bsz, seq, TM_RGLRU)
            outs[6].append(h_last)
            outs[7].append(cbuf)
    return x.reshape(bsz, seq, D_MODEL), [jnp.stack(o, axis=0) for o in outs]


def _sample_trunk(x_sample, cache_k, cache_v, st, ab_layers, c_layers):
    bd = x_sample.shape[0]
    x = x_sample.reshape(bd, D_MODEL)
    cos_t, sin_t = _rope_tables(jnp.full((bd,), PAST_LEN, dtype=jnp.int32))
    zeros = jnp.zeros((bd, LANES), F32)
    outs = [[] for _ in range(8)]
    upds = []
    for li in range(len(ab_layers)):
        lw = ab_layers[li]
        proj = _norm_inproj(x, lw["norm_pre"], lw["w_in"], cos_t, sin_t, n_rope=2, tm=bd)
        qkv = proj[:, :3 * A_WIDTH].reshape(bd, 3 * A_HEADS, A_HEAD_DIM)
        oa = _decode_attn(qkv, cache_k, cache_v, li)
        dec, upd, n1, m1, buf = _mlstm_decode(proj, st[3][li], st[0], li, st[1][li], st[2][li], lw)
        upds.append(upd)
        new_k = qkv[:, A_HEADS:2 * A_HEADS].reshape(bd, 1, A_HEADS, A_HEAD_DIM)
        new_v = qkv[:, 2 * A_HEADS:].reshape(bd, 1, A_HEADS, A_HEAD_DIM)
        x = _ab_out_decode(x, proj, lw, oa.reshape(bd, A_WIDTH), dec)
        for j, o in zip((0, 1, 3, 4, 5), (new_k, new_v, n1, m1, buf)):
            outs[j].append(o)
        if li < len(c_layers):
            lc = c_layers[li]
            projc = _norm_inproj(x, lc["norm_pre"], lc["w_in"], zeros, zeros, n_rope=0, tm=bd)
            x, h, cbuf = _rglru_decode(x, projc, st[5][li], st[4][li], lc)
            outs[6].append(h)
            outs[7].append(cbuf)
    stacked = [None if j == 2 else jnp.stack(o, axis=0) for j, o in enumerate(outs)]
    stacked[2] = _mlstm_decode_cnew(st[0], upds)
    return x.reshape(bd, 1, D_MODEL), stacked


def kernel(x_prompt, x_sample, cache_k, cache_v, state_mlstm_c, state_mlstm_n, state_mlstm_m, state_mlstm_conv,
           state_rglru_h, state_rglru_conv, ab_norm_pre, ab_norm_post, ab_w_in, ab_w_out, mlstm_conv_w,
           mlstm_conv_b, mlstm_wq, mlstm_wk, mlstm_wv, mlstm_w_gates, mlstm_b_gates, mlstm_out_norm, mlstm_skip,
           c_norm_pre, c_norm_post, c_w_in, c_w_out, c_conv_w, c_conv_b, c_w_a, c_b_a, c_w_x, c_b_x, c_lambda):
    n_ab = ab_w_in.shape[0]
    n_c = c_w_in.shape[0]
    ab_layers = [_prep_ab(li, ab_norm_pre, ab_norm_post, ab_w_in, ab_w_out, mlstm_conv_w, mlstm_conv_b, mlstm_wq,
                          mlstm_wk, mlstm_wv, mlstm_w_gates, mlstm_b_gates, mlstm_out_norm, mlstm_skip)
                 for li in range(n_ab)]
    c_layers = [_prep_c(li, c_norm_pre, c_norm_post, c_w_in, c_w_out, c_conv_w, c_conv_b, c_w_a, c_b_a, c_w_x,
                        c_b_x, c_lambda) for li in range(n_c)]
    y_prompt, pst = _prompt_trunk(x_prompt, ab_layers, c_layers)
    st = (state_mlstm_c, state_mlstm_n, state_mlstm_m, state_mlstm_conv, state_rglru_h, state_rglru_conv)
    y_sample, sst = _sample_trunk(x_sample, cache_k, cache_v, st, ab_layers, c_layers)
    return (y_prompt, y_sample, *pst, *sst)
```

```python
import functools

import jax
import jax.numpy as jnp
from jax import lax
from jax.experimental import pallas as pl
from jax.experimental.pallas import tpu as pltpu

F32 = jnp.float32
BF16 = jnp.bfloat16

D_MODEL = 1024
PAST_LEN = 2048
A_HEADS = 8
A_HEAD_DIM = 128
A_WIDTH = 1024
A_PATTERNS = ((128, 1), (512, 4), (2048, 16))
A_MAX_WINDOW = 2048
ROPE_THETA = 10000.0
B_HEADS = 4
B_WIDTH = 1024
B_HEAD_DIM = 256
B_QKV_BLOCK = 4
B_CHUNK = 128
C_WIDTH = 1536
C_BLOCKS = 12
C_BLOCK = 128
RG_LRU_C = 8.0
CONV_WIDTH = 4
AB_IN = 6 * 1024
C_IN = 2 * C_WIDTH
NORM_EPS = 1e-6

LANES = 128
SUBLANES = 8
VMEM_LIMIT = 56 * 1024 * 1024

TM_INPROJ = 256
TM_RGLRU = 512
ATT_TQ = A_MAX_WINDOW
ATT_BLK = 128
ATT_QB = 128
NB_DECODE_ATTN = 4
NB_DECODE_STATE = 32


def _cparams(*sem):
    return pltpu.CompilerParams(dimension_semantics=sem, vmem_limit_bytes=VMEM_LIMIT)


def _sigmoid(x):
    return 0.5 * jnp.tanh(0.5 * x) + 0.5


def _causal_conv(x, xs_scr, cw_ref, cb_ref):
    t, w = x.shape
    halo = SUBLANES
    pitch = 2 * (t + halo)
    parts = []
    for g in range(w // LANES):
        ls = slice(g * LANES, (g + 1) * LANES)
        base = g * pitch
        xg = x[:, ls]
        xs_scr[pl.ds(base + 2 * halo, t, stride=2), :] = xg
        acc = xg * cw_ref[CONV_WIDTH - 1:CONV_WIDTH, ls] + cb_ref[:, ls]
        for s in range(1, CONV_WIDTH):
            acc = acc + xs_scr[pl.ds(base + 2 * (halo - s), t, stride=2), :] * cw_ref[CONV_WIDTH - 1 - s:CONV_WIDTH - s, ls]
        xs_scr[pl.ds(base, halo, stride=2), :] = xg[t - halo:t, :]
        parts.append(acc)
    return jnp.concatenate(parts, axis=1)


def _silu(x):
    return x * _sigmoid(x)


def _log_sigmoid(x):
    return jnp.minimum(x, 0.0) - jnp.log1p(jnp.exp(-jnp.abs(x)))


def _softplus(x):
    return jnp.maximum(x, 0.0) + jnp.log1p(jnp.exp(-jnp.abs(x)))


def _rmsnorm(x, g):
    return x * lax.rsqrt(jnp.mean(x * x, axis=-1, keepdims=True) + NORM_EPS) * g


def _dot(a, b):
    return jnp.dot(a, b, preferred_element_type=F32)


def _dot_nt(a, b):
    return lax.dot_general(a, b, (((1,), (1,)), ((), ())), preferred_element_type=F32)


def _dot_tn(a, b):
    return lax.dot_general(a, b, (((0,), (0,)), ((), ())), preferred_element_type=F32)


def _inproj_kernel(x_ref, g_ref, w_ref, cos_ref, sin_ref, o_ref, *, n_rope):
    h = _rmsnorm(x_ref[...], g_ref[...]).astype(BF16)
    group = A_WIDTH
    for j in range(w_ref.shape[1] // group):
        p = _dot(h, w_ref[:, j * group:(j + 1) * group])
        if j < n_rope:
            c = cos_ref[...]
            s = sin_ref[...]
            for hd in range(group // A_HEAD_DIM):
                ph = p[:, hd * A_HEAD_DIM:(hd + 1) * A_HEAD_DIM]
                lo = j * group + hd * A_HEAD_DIM
                o_ref[:, lo:lo + A_HEAD_DIM] = ph * c + pltpu.roll(ph, A_HEAD_DIM // 2, 1) * s
        else:
            o_ref[:, j * group:(j + 1) * group] = p


def _norm_inproj(x, g, w_bf16, cos_t, sin_t, *, n_rope, tm):
    t, d = x.shape
    n = w_bf16.shape[1]
    n_pos_tiles = cos_t.shape[0] // tm
    return pl.pallas_call(
        functools.partial(_inproj_kernel, n_rope=n_rope),
        grid=(t // tm,),
        in_specs=[
            pl.BlockSpec((tm, d), lambda i: (i, 0)),
            pl.BlockSpec((1, d), lambda i: (0, 0)),
            pl.BlockSpec((d, n), lambda i: (0, 0)),
            pl.BlockSpec((tm, LANES), lambda i: (i % n_pos_tiles, 0)),
            pl.BlockSpec((tm, LANES), lambda i: (i % n_pos_tiles, 0)),
        ],
        out_specs=pl.BlockSpec((tm, n), lambda i: (i, 0)),
        out_shape=jax.ShapeDtypeStruct((t, n), F32),
        compiler_params=_cparams("parallel"),
        name="norm_inproj",
    )(x, g.reshape(1, d), w_bf16, cos_t, sin_t)


def _attend(q, k, v, bias):
    sc = _dot_nt(q, k) * (A_HEAD_DIM ** -0.5) + bias
    m = jnp.max(sc, axis=-1, keepdims=True)
    p = jnp.exp(sc - m)
    l = jnp.sum(p, axis=-1, keepdims=True)
    o = _dot(p.astype(BF16), v) / l
    return o, jnp.broadcast_to(m + jnp.log(l), (q.shape[0], LANES))


def _attn_kernel(q_ref, k_ref, v_ref, o_ref, d4f, q1, q4, q16, k1e, k4e, k16e, v1e, v4e, v16e,
                 o4n, o16n, l4n, l16n):
    i = pl.program_id(2)
    blk = ATT_BLK
    tq = ATT_TQ
    n4 = tq // 4
    n16 = tq // 16

    @pl.when(i == 0)
    def _():
        for r in (k1e, v1e):
            r[0:blk, :] = jnp.zeros((blk, LANES), BF16)
        for r in (k4e, v4e, k16e, v16e):
            r[:, 0:blk, :] = jnp.zeros((r.shape[0], blk, LANES), BF16)

    @pl.when(i > 0)
    def _():
        for r in (k1e, v1e):
            r[0:blk, :] = r[tq:tq + blk, :]
        for r in (k4e, v4e):
            r[:, 0:blk, :] = r[:, n4:n4 + blk, :]
        for r in (k16e, v16e):
            r[:, 0:blk, :] = r[:, n16:n16 + blk, :]

    def split(src_ref, d1, d4, d16, off):
        def natural(c, carry):
            r0 = pl.multiple_of(c * 2 * blk, 2 * blk)
            d1[pl.ds(off + r0, 2 * blk), :] = src_ref[0, pl.ds(r0, 2 * blk), :].astype(BF16)
            return carry
        lax.fori_loop(0, tq // (2 * blk), natural, 0)
        for r4 in range(4):
            for c in range(n4 // blk):
                x = src_ref[0, pl.ds(r4 + 4 * c * blk, blk, stride=4), :]
                d4f[r4 * n4 + c * blk:r4 * n4 + (c + 1) * blk, :] = x
                d4[r4, off + c * blk:off + (c + 1) * blk, :] = x.astype(BF16)
        for r4 in range(4):
            for a in range(4):
                x = d4f[pl.ds(r4 * n4 + a, n16, stride=4), :]
                d16[r4 + 4 * a, off:off + n16, :] = x.astype(BF16)

    split(q_ref, q1, q4, q16, 0)
    split(k_ref, k1e, k4e, k16e, blk)
    split(v_ref, v1e, v4e, v16e, blk)

    qb = ATT_QB
    row = lax.broadcasted_iota(jnp.int32, (qb, qb + blk), 0)
    col = lax.broadcasted_iota(jnp.int32, (qb, qb + blk), 1)
    bias = jnp.where((col >= row) & (col - row <= blk), 0.0, -jnp.inf)
    bias_at = [jnp.where(col >= jnp.where(i == 0, blk - s * qb, 0), bias, -jnp.inf) for s in range(blk // qb)]
    bias_at += [bias] * (tq // qb)

    for r in range(16):
        for s in range(n16 // qb):
            win = slice(s * qb, (s + 1) * qb + blk)
            o, lse = _attend(q16[r, s * qb:(s + 1) * qb, :], k16e[r, win, :], v16e[r, win, :], bias_at[s])
            o16n[pl.ds(16 * s * qb + r, qb, stride=16), :] = o
            l16n[pl.ds(16 * s * qb + r, qb, stride=16), :] = lse

    for r4 in range(4):
        for s in range(n4 // qb):
            win = slice(s * qb, (s + 1) * qb + blk)
            o, lse = _attend(q4[r4, s * qb:(s + 1) * qb, :], k4e[r4, win, :], v4e[r4, win, :], bias_at[s])
            o4n[pl.ds(4 * s * qb + r4, qb, stride=4), :] = o
            l4n[pl.ds(4 * s * qb + r4, qb, stride=4), :] = lse

    for s in range(tq // qb):
        rows = slice(s * qb, (s + 1) * qb)
        win = slice(s * qb, (s + 1) * qb + blk)
        o1, l1 = _attend(q1[rows, :], k1e[win, :], v1e[win, :], bias_at[s])
        l4, l16 = l4n[rows, :], l16n[rows, :]
        mx = jnp.maximum(jnp.maximum(l1, l4), l16)
        e1, e4, e16 = jnp.exp(l1 - mx), jnp.exp(l4 - mx), jnp.exp(l16 - mx)
        o_ref[0, rows, :] = (o1 * e1 + o4n[rows, :] * e4 + o16n[rows, :] * e16) / (e1 + e4 + e16)


def _dilated_attn(proj, bsz, seq):
    tq, blk = ATT_TQ, ATT_BLK
    view = proj.reshape(bsz, seq, AB_IN)
    col = lambda c: pl.BlockSpec((1, tq, A_HEAD_DIM), lambda b, h, i: (b, i, c * A_HEADS + h))
    bf = lambda *s: pltpu.VMEM(s, BF16)
    nat = pltpu.VMEM((tq, LANES), F32)
    out = pl.pallas_call(
        _attn_kernel,
        grid=(bsz, A_HEADS, seq // tq),
        in_specs=[col(0), col(1), col(2)],
        out_specs=pl.BlockSpec((1, tq, A_HEAD_DIM), lambda b, h, i: (b, i, h)),
        out_shape=jax.ShapeDtypeStruct((bsz, seq, A_WIDTH), F32),
        scratch_shapes=[nat,
                        bf(tq, LANES), bf(4, tq // 4, LANES), bf(16, tq // 16, LANES),
                        bf(tq + blk, LANES), bf(4, tq // 4 + blk, LANES), bf(16, tq // 16 + blk, LANES),
                        bf(tq + blk, LANES), bf(4, tq // 4 + blk, LANES), bf(16, tq // 16 + blk, LANES),
                        nat, nat, nat, nat],
        compiler_params=_cparams("parallel", "parallel", "arbitrary"),
        name="dilated_attn",
    )(view, view, view)
    return out.reshape(bsz * seq, A_WIDTH)


def _decode_attn_kernel(q_ref, kn_ref, vn_ref, k1_ref, k4_ref, k16_ref, v1_ref, v4_ref, v16_ref, o_ref, *, nb):
    scale = A_HEAD_DIM ** -0.5
    for bb in range(nb):
        q = q_ref[bb]
        kn = kn_ref[bb]
        vn = vn_ref[bb]
        s_new = jnp.sum(q * kn, axis=-1, keepdims=True) * scale
        outs, lses = [], []
        for k_ref, v_ref in ((k1_ref, v1_ref), (k4_ref, v4_ref), (k16_ref, v16_ref)):
            kk = k_ref[0, bb, :, 0]
            vv = v_ref[0, bb, :, 0]
            sc = jnp.sum(kk * q[None], axis=-1, keepdims=True) * scale
            m = jnp.maximum(jnp.max(sc, axis=0), s_new)
            p = jnp.exp(sc - m[None])
            p_new = jnp.exp(s_new - m)
            l = jnp.sum(p, axis=0) + p_new
            outs.append((jnp.sum(p * vv, axis=0) + p_new * vn) / l)
            lses.append(m + jnp.log(l))
        mx = jnp.maximum(jnp.maximum(lses[0], lses[1]), lses[2])
        e = [jnp.exp(x - mx) for x in lses]
        tot = e[0] + e[1] + e[2]
        o_ref[bb] = (outs[0] * e[0] + outs[1] * e[1] + outs[2] * e[2]) / tot


def _decode_attn(qkv, cache_k, cache_v, li):
    bd = qkv.shape[0]
    n_layers, _, win = cache_k.shape[:3]
    nb = NB_DECODE_ATTN
    new = lambda c: pl.BlockSpec((nb, A_HEADS, A_HEAD_DIM), lambda i: (i, c, 0))
    specs, views = [], []
    for cache in (cache_k, cache_v):
        for window, dil in A_PATTERNS:
            rows = win // dil
            n_keys = window // dil
            views.append(cache.reshape(n_layers, bd, rows, dil, A_HEADS, A_HEAD_DIM))
            specs.append(pl.BlockSpec((1, nb, n_keys, 1, A_HEADS, A_HEAD_DIM),
                                      lambda i, rb=rows // n_keys - 1: (li, i, rb, 0, 0, 0)))
    return pl.pallas_call(
        functools.partial(_decode_attn_kernel, nb=nb),
        grid=(bd // nb,),
        in_specs=[new(0), new(1), new(2)] + specs,
        out_specs=pl.BlockSpec((nb, A_HEADS, A_HEAD_DIM), lambda i: (i, 0, 0)),
        out_shape=jax.ShapeDtypeStruct((bd, A_HEADS, A_HEAD_DIM), F32),
        compiler_params=_cparams("parallel"),
        name="decode_attn",
    )(qkv, qkv, qkv, *views)


def _blockdiag_dense(w):
    nb = w.shape[0]
    per = LANES // B_QKV_BLOCK
    w4 = w.reshape(nb // per, per, B_QKV_BLOCK, B_QKV_BLOCK)
    return jnp.einsum('gnio,nm->gnimo', w4, jnp.eye(per, dtype=w.dtype)).reshape(nb // per, LANES, LANES)


def _hi_lo(x):
    hi = x.astype(BF16)
    return hi, (x - hi.astype(F32)).astype(BF16)


def _blockdiag_mxu(x, whi_ref, wlo_ref):
    outs = []
    for g in range(x.shape[1] // LANES):
        hi, lo = _hi_lo(x[:, g * LANES:(g + 1) * LANES])
        outs.append(_dot(hi, whi_ref[g]) + _dot(lo, whi_ref[g]) + _dot(hi, wlo_ref[g]))
    return outs


def _mlstm_qkv(xb, xc, wqk_hi, wqk_lo, wv_hi, wv_lo):
    qk = _blockdiag_mxu(xc, wqk_hi, wqk_lo)
    qb = jnp.concatenate([o[:, :LANES] for o in qk], axis=1)
    kb = jnp.concatenate([o[:, LANES:] for o in qk], axis=1)
    vb = jnp.concatenate(_blockdiag_mxu(xb, wv_hi, wv_lo), axis=1)
    return qb, kb, vb


def _head_out(hb_raw, xc, on_ref, sk_ref):
    parts = []
    for h in range(B_HEADS):
        xh = hb_raw[:, h * B_HEAD_DIM:(h + 1) * B_HEAD_DIM]
        xcn = xh - jnp.mean(xh, axis=-1, keepdims=True)
        parts.append(xcn * lax.rsqrt(jnp.mean(xcn * xcn, axis=-1, keepdims=True) + 1e-5))
    return jnp.concatenate(parts, axis=1) * on_ref[...] + sk_ref[...] * xc


def _cumsum_lanes(x):
    t = x.shape[1]
    idx = lax.broadcasted_iota(jnp.int32, x.shape, 1)
    k = 1
    while k < t:
        x = x + jnp.where(idx >= k, pltpu.roll(x, k, 1), 0.0)
        k *= 2
    return x


def _mlstm_chunk(xc, qb, kb, qh, vh, gt, hraw_scr, c_scr, n_scr, m_scr, on_ref, sk_ref):
    t = B_CHUNK
    ng = 2 * B_HEADS
    brow = _cumsum_lanes(_log_sigmoid(gt))
    gcol = jnp.concatenate([gt, brow, jnp.zeros((t - 2 * ng, t), F32)], axis=0).T
    ri = lax.broadcasted_iota(jnp.int32, (t, t), 0)
    cj = lax.broadcasted_iota(jnp.int32, (t, t), 1)
    causal = cj <= ri
    kscale = B_HEAD_DIM ** -0.5
    for h in range(B_HEADS):
        sl = slice(h * B_HEAD_DIM, (h + 1) * B_HEAD_DIM)
        q = qb[:, sl]
        k = kb[:, sl] * kscale
        qhh = qh[:, sl]
        khh = k.astype(BF16)
        vhh = vh[:, sl]
        logi_c = gcol[:, h:h + 1]
        b_c = gcol[:, ng + B_HEADS + h:ng + B_HEADS + h + 1]
        logi_r = gt[h:h + 1, :]
        b_r = brow[B_HEADS + h:B_HEADS + h + 1, :]
        m0 = m_scr[h:h + 1, 0:1]
        n0 = n_scr[h:h + 1, :]
        c0 = c_scr[h]
        dmat = jnp.where(causal, b_c - b_r + logi_r, -jnp.inf)
        inter = b_c + m0
        m = jnp.maximum(jnp.max(dmat, axis=1, keepdims=True), inter)
        w_intra = jnp.exp(dmat - m)
        w_inter = jnp.exp(inter - m)
        s = _dot_nt(qhh, khh) * w_intra
        num = _dot(s.astype(BF16), vhh) + w_inter * _dot(qhh, c0.astype(BF16))
        den = jnp.sum(s, axis=1, keepdims=True) + w_inter * jnp.sum(q * n0, axis=1, keepdims=True)
        hraw_scr[:, sl] = num / jnp.maximum(jnp.abs(den), jnp.exp(-m))
        bl = b_c[t - 1:t, :]
        dl = bl - b_c + logi_c
        m_new = jnp.maximum(bl + m0, jnp.max(dl, axis=0, keepdims=True))
        kw = jnp.exp(dl - m_new) * k
        carry = jnp.exp(bl + m0 - m_new)
        c_scr[h] = carry * c0 + _dot_tn(kw.astype(BF16), vhh)
        n_scr[h:h + 1, :] = carry * n0 + jnp.sum(kw, axis=0, keepdims=True)
        m_scr[h:h + 1, :] = jnp.broadcast_to(m_new, (1, LANES))
    return _head_out(hraw_scr[...], xc, on_ref, sk_ref)


def _gate_project(oa, hb, za, zb, x, w_ref, g_ref):
    ya = oa * _silu(za)
    yb = hb * _silu(zb)
    y = _dot(ya.astype(BF16), w_ref[0:A_WIDTH, :]) + _dot(yb.astype(BF16), w_ref[A_WIDTH:A_WIDTH + B_WIDTH, :])
    return x + _rmsnorm(y, g_ref[...])


def _mlstm_prompt_kernel(xb_ref, oa_ref, za_ref, zb_ref, x_ref, cw_ref, cb_ref, wqk_hi, wqk_lo, wv_hi, wv_lo,
                         wgt_ref, bgt_ref, on_ref, sk_ref, w_ref, g_ref,
                         o_ref, c_out, n_out, m_out, buf_out,
                         xx_scr, hraw_scr, c_scr, n_scr, m_scr):
    ci = pl.program_id(0)
    nc = pl.num_programs(0)
    t = B_CHUNK
    halo = SUBLANES
    bsz = xb_ref.shape[0]

    @pl.when(ci == 0)
    def _():
        xx_scr[...] = jnp.zeros_like(xx_scr)
        c_scr[...] = jnp.zeros_like(c_scr)
        n_scr[...] = jnp.zeros_like(n_scr)
        m_scr[...] = jnp.zeros_like(m_scr)

    rows = lambda r: r[...].reshape(bsz * t, r.shape[-1])
    x = rows(xb_ref)
    xc = jnp.concatenate([_silu(_causal_conv(xb_ref[b], xx_scr.at[b], cw_ref, cb_ref)) for b in range(bsz)],
                         axis=0)
    qb, kb, vb = _mlstm_qkv(x, xc, wqk_hi, wqk_lo, wv_hi, wv_lo)
    qh, kh, vh = qb.astype(BF16), kb.astype(BF16), vb.astype(BF16)
    gt = (_dot_nt(wgt_ref[:, 0:B_WIDTH], qh) + _dot_nt(wgt_ref[:, B_WIDTH:2 * B_WIDTH], kh)
          + _dot_nt(wgt_ref[:, 2 * B_WIDTH:3 * B_WIDTH], vh) + bgt_ref[...])
    hb = []
    for b in range(bsz):
        rs = slice(b * t, (b + 1) * t)
        hb.append(_mlstm_chunk(xc[rs], qb[rs], kb[rs], qh[rs], vh[rs], gt[:, rs], hraw_scr.at[b], c_scr.at[b],
                               n_scr.at[b], m_scr.at[b], on_ref, sk_ref))
    y = _gate_project(rows(oa_ref), jnp.concatenate(hb, axis=0), rows(za_ref), rows(zb_ref), rows(x_ref),
                      w_ref, g_ref)
    o_ref[...] = y.reshape(bsz, t, D_MODEL)

    @pl.when(ci == nc - 1)
    def _():
        c_out[...] = c_scr[...]
        n_out[...] = n_scr[:, 0:B_HEADS, :]
        lane = lax.broadcasted_iota(jnp.int32, (1, LANES), 1)
        for b in range(bsz):
            mt = jnp.zeros((1, LANES), F32)
            for h in range(B_HEADS):
                mt = jnp.where(lane == h, m_scr[b, h:h + 1, :], mt)
            m_out[b] = mt
            buf_out[b] = xb_ref[b, t - halo:t, :]


def _mlstm_prompt(x, proj, oa, lw, bsz, seq):
    t = B_CHUNK
    view = proj.reshape(bsz, seq, AB_IN)
    full = lambda a: pl.BlockSpec(a.shape, lambda c: (0,) * a.ndim)
    pcol = lambda col: pl.BlockSpec((bsz, t, B_WIDTH), lambda c: (0, c, col))
    row = pl.BlockSpec((bsz, t, D_MODEL), lambda c: (0, c, 0))
    whole = lambda *s: pl.BlockSpec(s, lambda c: (0,) * len(s))
    params = [lw["conv_w"], lw["conv_b"], lw["wqk_hi"], lw["wqk_lo"], lw["wv_hi"], lw["wv_lo"], lw["wgt"],
              lw["bgt"], lw["out_norm"], lw["skip"], lw["w_out"], lw["norm_post"]]
    sds = jax.ShapeDtypeStruct
    y, c, n, m, buf = pl.pallas_call(
        _mlstm_prompt_kernel,
        grid=(seq // t,),
        in_specs=[pcol(4), row, pcol(3), pcol(5), row] + [full(a) for a in params],
        out_specs=[row,
                   whole(bsz, B_HEADS, B_HEAD_DIM, B_HEAD_DIM),
                   whole(bsz, B_HEADS, B_HEAD_DIM),
                   whole(bsz, 1, LANES),
                   whole(bsz, SUBLANES, B_WIDTH)],
        out_shape=[sds((bsz, seq, D_MODEL), F32),
                   sds((bsz, B_HEADS, B_HEAD_DIM, B_HEAD_DIM), F32),
                   sds((bsz, B_HEADS, B_HEAD_DIM), F32),
                   sds((bsz, 1, LANES), F32),
                   sds((bsz, SUBLANES, B_WIDTH), F32)],
        scratch_shapes=[pltpu.VMEM((bsz, (B_WIDTH // LANES) * 2 * (t + SUBLANES), LANES), F32),
                        pltpu.VMEM((bsz, t, B_WIDTH), F32),
                        pltpu.VMEM((bsz, B_HEADS, B_HEAD_DIM, B_HEAD_DIM), F32),
                        pltpu.VMEM((bsz, SUBLANES, B_HEAD_DIM), F32),
                        pltpu.VMEM((bsz, SUBLANES, LANES), F32)],
        compiler_params=_cparams("arbitrary"),
        name="mlstm_prompt",
    )(view, oa.reshape(bsz, seq, A_WIDTH), view, view, x.reshape(bsz, seq, D_MODEL), *params)
    return (y.reshape(bsz * seq, D_MODEL), c, n, m[:, 0, :B_HEADS], buf[:, SUBLANES - (CONV_WIDTH - 1):, :])


def _mlstm_decode_pre_kernel(xb_ref, b0_ref, b1_ref, b2_ref, cw_ref, cb_ref, wqk_hi, wqk_lo, wv_hi, wv_lo, wg_ref,
                             bg_ref, n0_ref, m0_ref,
                             xc_ref, v_ref, sv_ref, wi_ref, den_ref, n_out, m_out, buf_out, qt_ref, kwt_ref):
    x = xb_ref[...]
    xconv = (x * cw_ref[3:4, :] + b2_ref[...] * cw_ref[2:3, :] + b1_ref[...] * cw_ref[1:2, :]
             + b0_ref[...] * cw_ref[0:1, :] + cb_ref[...])
    xc = _silu(xconv)
    xc_ref[...] = xc
    buf_out[:, 0:B_WIDTH] = b1_ref[...]
    buf_out[:, B_WIDTH:2 * B_WIDTH] = b2_ref[...]
    buf_out[:, 2 * B_WIDTH:3 * B_WIDTH] = x
    qb, kb, vb = _mlstm_qkv(x, xc, wqk_hi, wqk_lo, wv_hi, wv_lo)
    g = (_dot(qb.astype(BF16), wg_ref[0:B_WIDTH, :]) + _dot(kb.astype(BF16), wg_ref[B_WIDTH:2 * B_WIDTH, :])
         + _dot(vb.astype(BF16), wg_ref[2 * B_WIDTH:3 * B_WIDTH, :]) + bg_ref[...])
    v_ref[...] = vb
    logf = _log_sigmoid(g)
    rows = x.shape[0]
    lane = lax.broadcasted_iota(jnp.int32, (rows, LANES), 1)
    m_tile = jnp.zeros((rows, LANES), F32)
    kscale = B_HEAD_DIM ** -0.5
    for h in range(B_HEADS):
        sl = slice(h * B_HEAD_DIM, (h + 1) * B_HEAD_DIM)
        logi = g[:, h:h + 1]
        fm = logf[:, B_HEADS + h:B_HEADS + h + 1] + m0_ref[:, h:h + 1]
        m = jnp.maximum(logi, fm)
        w_intra = jnp.exp(logi - m)
        w_inter = jnp.exp(fm - m)
        q = qb[:, sl]
        k = kb[:, sl] * kscale
        n0 = n0_ref[:, sl]
        s = jnp.sum(q * k, axis=1, keepdims=True) * w_intra
        den = s + w_inter * jnp.sum(q * n0, axis=1, keepdims=True)
        kw = w_intra * k
        sv_ref[:, sl] = s * vb[:, sl]
        wi_ref[:, sl] = jnp.broadcast_to(w_inter, q.shape)
        den_ref[:, sl] = jnp.broadcast_to(jnp.maximum(jnp.abs(den), jnp.exp(-m)), q.shape)
        n_out[:, sl] = w_inter * n0 + kw
        m_tile = jnp.where(lane == h, m, m_tile)
        qt_ref[h] = q.T
        kwt_ref[h] = kw.T
    m_out[...] = m_tile


def _column(mat_t, b):
    lane = lax.broadcasted_iota(jnp.int32, mat_t.shape, 1)
    return jnp.sum(jnp.where(lane == b, mat_t, 0.0), axis=1, keepdims=True)


def _mlstm_decode_qc_kernel(c0_ref, qt_ref, qc_ref, *, nb):
    i = pl.program_id(1)
    qt = qt_ref[0]
    for bb in range(nb):
        qc_ref[bb] = jnp.sum(c0_ref[0, bb, 0] * _column(qt, i * nb + bb), axis=0, keepdims=True)


def _mlstm_decode_cnew_kernel(c0_ref, kwt_ref, v_ref, wi_ref, c_out, *, nb):
    i = pl.program_id(2)
    kwt = kwt_ref[0, 0]
    for bb in range(nb):
        c_out[0, bb, 0] = wi_ref[0, bb] * c0_ref[0, bb, 0] + _column(kwt, i * nb + bb) * v_ref[0, bb]


def _mlstm_decode(proj, conv_state, state_c, li, n0, m0, lw):
    bd = proj.shape[0]
    w = B_WIDTH
    cs = conv_state.reshape(bd, (CONV_WIDTH - 1) * w)
    m0p = jnp.pad(m0, ((0, 0), (0, LANES - B_HEADS)))
    row = lambda c: pl.BlockSpec((bd, w), lambda i: (0, c))
    full = lambda a: pl.BlockSpec(a.shape, lambda i: (0,) * a.ndim)
    params = [lw["conv_w"], lw["conv_b"], lw["wqk_hi"], lw["wqk_lo"], lw["wv_hi"], lw["wv_lo"], lw["wg"], lw["bg"]]
    n0f = n0.reshape(bd, w)
    sds = jax.ShapeDtypeStruct
    tr = pl.BlockSpec((B_HEADS, B_HEAD_DIM, bd), lambda i: (0, 0, 0))
    outs = pl.pallas_call(
        _mlstm_decode_pre_kernel,
        grid=(1,),
        in_specs=[row(4), row(0), row(1), row(2)] + [full(a) for a in params] + [full(n0f), full(m0p)],
        out_specs=[row(0)] * 6 + [pl.BlockSpec((bd, LANES), lambda i: (0, 0)),
                                  pl.BlockSpec((bd, 3 * w), lambda i: (0, 0)), tr, tr],
        out_shape=[sds((bd, w), F32)] * 6 + [sds((bd, LANES), F32), sds((bd, 3 * w), F32),
                                             sds((B_HEADS, B_HEAD_DIM, bd), F32),
                                             sds((B_HEADS, B_HEAD_DIM, bd), F32)],
        compiler_params=_cparams("arbitrary"),
        name="mlstm_decode_pre",
    )(proj, cs, cs, cs, *params, n0f, m0p)
    xc, v, sv, wi, den, n_new, m_new, buf, qt, kwt = outs
    nb = NB_DECODE_STATE
    qc = pl.pallas_call(
        functools.partial(_mlstm_decode_qc_kernel, nb=nb),
        grid=(B_HEADS, bd // nb),
        in_specs=[pl.BlockSpec((1, nb, 1, B_HEAD_DIM, B_HEAD_DIM), lambda h, i: (li, i, h, 0, 0)),
                  pl.BlockSpec((1, B_HEAD_DIM, bd), lambda h, i: (h, 0, 0))],
        out_specs=pl.BlockSpec((nb, 1, B_HEAD_DIM), lambda h, i: (i, 0, h)),
        out_shape=sds((bd, 1, w), F32),
        compiler_params=_cparams("parallel", "arbitrary"),
        name="mlstm_decode_qc",
    )(state_c, qt)
    dec = dict(xc=xc, sv=sv, wi=wi, den=den, qc=qc.reshape(bd, w))
    upd = dict(kwt=kwt, v=v.reshape(bd, 1, w), wi=wi.reshape(bd, 1, w))
    return (dec, upd, n_new.reshape(bd, B_HEADS, B_HEAD_DIM), m_new[:, :B_HEADS],
            buf.reshape(bd, CONV_WIDTH - 1, w))


def _mlstm_decode_cnew(state_c, upds):
    n_layers, bd = state_c.shape[:2]
    nb = NB_DECODE_STATE
    kwt = jnp.stack([u["kwt"] for u in upds], axis=0)
    v = jnp.stack([u["v"] for u in upds], axis=0)
    wi = jnp.stack([u["wi"] for u in upds], axis=0)
    cblk = pl.BlockSpec((1, nb, 1, B_HEAD_DIM, B_HEAD_DIM), lambda l, h, i: (l, i, h, 0, 0))
    rowh = pl.BlockSpec((1, nb, 1, B_HEAD_DIM), lambda l, h, i: (l, i, 0, h))
    return pl.pallas_call(
        functools.partial(_mlstm_decode_cnew_kernel, nb=nb),
        grid=(n_layers, B_HEADS, bd // nb),
        in_specs=[cblk, pl.BlockSpec((1, 1, B_HEAD_DIM, bd), lambda l, h, i: (l, h, 0, 0)), rowh, rowh],
        out_specs=cblk,
        out_shape=jax.ShapeDtypeStruct(state_c.shape, F32),
        compiler_params=_cparams("parallel", "parallel", "arbitrary"),
        name="mlstm_decode_cnew",
    )(state_c, kwt, v, wi)


def _ab_out_decode_kernel(oa_ref, sv_ref, wi_ref, qc_ref, den_ref, xc_ref, za_ref, zb_ref, x_ref,
                          w_ref, on_ref, sk_ref, g_ref, o_ref):
    hb = _head_out((sv_ref[...] + wi_ref[...] * qc_ref[...]) / den_ref[...], xc_ref[...], on_ref, sk_ref)
    o_ref[...] = _gate_project(oa_ref[...], hb, za_ref[...], zb_ref[...], x_ref[...], w_ref, g_ref)


def _ab_out_decode(x, proj, lw, oa, dec):
    bd, w = x.shape
    row = pl.BlockSpec((bd, w), lambda i: (0, 0))
    pcol = lambda c: pl.BlockSpec((bd, w), lambda i: (0, c))
    full = lambda a: pl.BlockSpec(a.shape, lambda i: (0,) * a.ndim)
    params = [lw["w_out"], lw["out_norm"], lw["skip"], lw["norm_post"]]
    return pl.pallas_call(
        _ab_out_decode_kernel,
        grid=(1,),
        in_specs=[row] * 6 + [pcol(3), pcol(5), row] + [full(a) for a in params],
        out_specs=row,
        out_shape=jax.ShapeDtypeStruct((bd, w), F32),
        compiler_params=_cparams("arbitrary"),
        name="ab_out_decode",
    )(oa, dec["sv"], dec["wi"], dec["qc"], dec["den"], dec["xc"], proj, proj, x, *params)


def _rglru_gates(xconv, wa_ref, ba_ref, wx_ref, bx_ref, lam_ref):
    rs, is_ = [], []
    for n in range(C_BLOCKS):
        sl = slice(n * C_BLOCK, (n + 1) * C_BLOCK)
        xa = xconv[:, sl].astype(BF16)
        rs.append(_dot(xa, wa_ref[n]))
        is_.append(_dot(xa, wx_ref[n]))
    r = _sigmoid(jnp.concatenate(rs, axis=1) + ba_ref[...])
    ig = _sigmoid(jnp.concatenate(is_, axis=1) + bx_ref[...])
    a = jnp.exp(r * (-RG_LRU_C * _softplus(-lam_ref[...])))
    u = jnp.sqrt(1.0 - a * a) * (ig * xconv)
    return a, u


def _rglru_out(h, z, x, w_ref, g_ref):
    y = _dot((h * _silu(z)).astype(BF16), w_ref[...])
    return x + _rmsnorm(y, g_ref[...])


def _rglru_prompt_kernel(xc_ref, zc_ref, x_ref, cw_ref, cb_ref, wa_ref, ba_ref, wx_ref, bx_ref, lam_ref,
                         w_ref, g_ref, o_ref, h_out, buf_out, xx_scr, a_scr, u_scr, hc_scr, *, tm):
    ti = pl.program_id(1)
    nt = pl.num_programs(1)
    halo = SUBLANES

    @pl.when(ti == 0)
    def _():
        xx_scr[...] = jnp.zeros_like(xx_scr)
        hc_scr[...] = jnp.zeros_like(hc_scr)

    x = xc_ref[0]
    xconv = _causal_conv(x, xx_scr, cw_ref, cb_ref)
    a, u = _rglru_gates(xconv, wa_ref, ba_ref, wx_ref, bx_ref, lam_ref)
    a_scr[...] = a
    u_scr[...] = u
    ridx = lax.broadcasted_iota(jnp.int32, (SUBLANES, C_WIDTH), 0)

    def body(r, hprev):
        r8 = pl.multiple_of(r * SUBLANES, SUBLANES)
        aa = a_scr[pl.ds(r8, SUBLANES), :]
        uu = u_scr[pl.ds(r8, SUBLANES), :]
        for sft in (1, 2, 4):
            keep = ridx >= sft
            a_sh = jnp.where(keep, pltpu.roll(aa, sft, 0), 1.0)
            u_sh = jnp.where(keep, pltpu.roll(uu, sft, 0), 0.0)
            uu = aa * u_sh + uu
            aa = aa * a_sh
        hblk = aa * hprev + uu
        u_scr[pl.ds(r8, SUBLANES), :] = hblk
        return hblk[SUBLANES - 1:SUBLANES, :]

    hlast = lax.fori_loop(0, tm // SUBLANES, body, hc_scr[0:1, :])
    hc_scr[0:1, :] = hlast
    o_ref[0] = _rglru_out(u_scr[...], zc_ref[0], x_ref[0], w_ref, g_ref)

    @pl.when(ti == nt - 1)
    def _():
        h_out[0] = hlast
        buf_out[0] = x[tm - halo:tm, :]


def _rglru_decode_kernel(xc_ref, zc_ref, x_ref, b0_ref, b1_ref, b2_ref, h0_ref, cw_ref, cb_ref, wa_ref, ba_ref,
                         wx_ref, bx_ref, lam_ref, w_ref, g_ref, o_ref, h_out, buf_out):
    x = xc_ref[...]
    xconv = (x * cw_ref[3:4, :] + b2_ref[...] * cw_ref[2:3, :] + b1_ref[...] * cw_ref[1:2, :]
             + b0_ref[...] * cw_ref[0:1, :] + cb_ref[...])
    a, u = _rglru_gates(xconv, wa_ref, ba_ref, wx_ref, bx_ref, lam_ref)
    h = a * h0_ref[...] + u
    h_out[...] = h
    buf_out[:, 0:C_WIDTH] = b1_ref[...]
    buf_out[:, C_WIDTH:2 * C_WIDTH] = b2_ref[...]
    buf_out[:, 2 * C_WIDTH:3 * C_WIDTH] = x
    o_ref[...] = _rglru_out(h, zc_ref[...], x_ref[...], w_ref, g_ref)


def _c_params(lw):
    return [lw["conv_w"], lw["conv_b"], lw["w_a"], lw["b_a"], lw["w_x"], lw["b_x"], lw["lam"],
            lw["w_out"], lw["norm_post"]]


def _rglru_prompt(x, proj, lw, bsz, seq, tm):
    cw = C_WIDTH
    pv = proj.reshape(bsz, seq, C_IN)
    xv = x.reshape(bsz, seq, D_MODEL)
    params = _c_params(lw)
    full = lambda a: pl.BlockSpec(a.shape, lambda b, i: (0,) * a.ndim)
    sds = jax.ShapeDtypeStruct
    y, h_last, buf = pl.pallas_call(
        functools.partial(_rglru_prompt_kernel, tm=tm),
        grid=(bsz, seq // tm),
        in_specs=[pl.BlockSpec((1, tm, cw), lambda b, i: (b, i, 0)),
                  pl.BlockSpec((1, tm, cw), lambda b, i: (b, i, 1)),
                  pl.BlockSpec((1, tm, D_MODEL), lambda b, i: (b, i, 0))] + [full(a) for a in params],
        out_specs=[pl.BlockSpec((1, tm, D_MODEL), lambda b, i: (b, i, 0)),
                   pl.BlockSpec((1, 1, cw), lambda b, i: (b, 0, 0)),
                   pl.BlockSpec((1, SUBLANES, cw), lambda b, i: (b, 0, 0))],
        out_shape=[sds((bsz, seq, D_MODEL), F32), sds((bsz, 1, cw), F32), sds((bsz, SUBLANES, cw), F32)],
        scratch_shapes=[pltpu.VMEM(((cw // LANES) * 2 * (tm + SUBLANES), LANES), F32), pltpu.VMEM((tm, cw), F32),
                        pltpu.VMEM((tm, cw), F32), pltpu.VMEM((SUBLANES, cw), F32)],
        compiler_params=_cparams("parallel", "arbitrary"),
        name="rglru_prompt",
    )(pv, pv, xv, *params)
    return y.reshape(bsz * seq, D_MODEL), h_last[:, 0, :], buf[:, SUBLANES - (CONV_WIDTH - 1):, :]


def _rglru_decode(x, proj, conv_state, h0, lw):
    bd = x.shape[0]
    cw = C_WIDTH
    cs = conv_state.reshape(bd, (CONV_WIDTH - 1) * cw)
    params = _c_params(lw)
    col = lambda c: pl.BlockSpec((bd, cw), lambda i: (0, c))
    full = lambda a: pl.BlockSpec(a.shape, lambda i: (0,) * a.ndim)
    sds = jax.ShapeDtypeStruct
    y, h, buf = pl.pallas_call(
        _rglru_decode_kernel,
        grid=(1,),
        in_specs=[col(0), col(1), full(x), col(0), col(1), col(2), full(h0)] + [full(a) for a in params],
        out_specs=[pl.BlockSpec((bd, D_MODEL), lambda i: (0, 0)), pl.BlockSpec((bd, cw), lambda i: (0, 0)),
                   pl.BlockSpec((bd, 3 * cw), lambda i: (0, 0))],
        out_shape=[sds((bd, D_MODEL), F32), sds((bd, cw), F32), sds((bd, 3 * cw), F32)],
        compiler_params=_cparams("arbitrary"),
        name="rglru_decode",
    )(proj, proj, x, cs, cs, cs, h0, *params)
    return y, h, buf.reshape(bd, CONV_WIDTH - 1, cw)


def _rope_tables(pos):
    half = A_HEAD_DIM // 2
    lane = jnp.arange(A_HEAD_DIM)
    freqs = ROPE_THETA ** (-(lane % half).astype(F32) / half)
    ang = pos.astype(F32)[:, None] * freqs[None, :]
    sign = jnp.where(lane < half, -1.0, 1.0).astype(F32)
    return jnp.cos(ang), jnp.sin(ang) * sign[None, :]


def _prep_ab(li, ab_norm_pre, ab_norm_post, ab_w_in, ab_w_out, mlstm_conv_w, mlstm_conv_b, mlstm_wq, mlstm_wk,
             mlstm_wv, mlstm_w_gates, mlstm_b_gates, mlstm_out_norm, mlstm_skip):
    ng = 2 * B_HEADS
    wg = jnp.pad(mlstm_w_gates[li], ((0, 0), (0, LANES - ng))).astype(BF16)
    wqk = jnp.concatenate([_blockdiag_dense(mlstm_wq[li]), _blockdiag_dense(mlstm_wk[li])], axis=2)
    wqk_hi, wqk_lo = _hi_lo(wqk)
    wv_hi, wv_lo = _hi_lo(_blockdiag_dense(mlstm_wv[li]))
    return dict(
        norm_pre=ab_norm_pre[li], norm_post=ab_norm_post[li].reshape(1, D_MODEL),
        w_in=ab_w_in[li].astype(BF16), w_out=ab_w_out[li].astype(BF16),
        conv_w=mlstm_conv_w[li], conv_b=mlstm_conv_b[li].reshape(1, B_WIDTH),
        wqk_hi=wqk_hi, wqk_lo=wqk_lo, wv_hi=wv_hi, wv_lo=wv_lo,
        wg=wg, wgt=mlstm_w_gates[li].T.astype(BF16),
        bg=jnp.pad(mlstm_b_gates[li], (0, LANES - ng)).reshape(1, LANES),
        bgt=mlstm_b_gates[li].reshape(ng, 1),
        out_norm=mlstm_out_norm[li].reshape(1, B_WIDTH), skip=mlstm_skip[li].reshape(1, B_WIDTH))


def _prep_c(li, c_norm_pre, c_norm_post, c_w_in, c_w_out, c_conv_w, c_conv_b, c_w_a, c_b_a, c_w_x, c_b_x,
            c_lambda):
    return dict(
        norm_pre=c_norm_pre[li], norm_post=c_norm_post[li].reshape(1, D_MODEL),
        w_in=c_w_in[li].astype(BF16), w_out=c_w_out[li].astype(BF16),
        conv_w=c_conv_w[li], conv_b=c_conv_b[li].reshape(1, C_WIDTH),
        w_a=c_w_a[li].astype(BF16), b_a=c_b_a[li].reshape(1, C_WIDTH),
        w_x=c_w_x[li].astype(BF16), b_x=c_b_x[li].reshape(1, C_WIDTH),
        lam=c_lambda[li].reshape(1, C_WIDTH))


def _prompt_trunk(x_prompt, ab_layers, c_layers):
    bsz, seq, _ = x_prompt.shape
    x = x_prompt.reshape(bsz * seq, D_MODEL)
    cos_t, sin_t = _rope_tables(jnp.arange(seq, dtype=jnp.int32))
    zeros = jnp.zeros((seq, LANES), F32)
    outs = [[] for _ in range(8)]
    keep = min(A_MAX_WINDOW, seq)
    for li in range(len(ab_layers)):
        lw = ab_layers[li]
        proj = _norm_inproj(x, lw["norm_pre"], lw["w_in"], cos_t, sin_t, n_rope=2, tm=TM_INPROJ)
        oa = _dilated_attn(proj, bsz, seq)
        x, c1, n1, m1, buf = _mlstm_prompt(x, proj, oa, lw, bsz, seq)
        pv = proj.reshape(bsz, seq, AB_IN)
        new_k = pv[:, seq - keep:, A_WIDTH:2 * A_WIDTH].reshape(bsz, keep, A_HEADS, A_HEAD_DIM)
        new_v = pv[:, seq - keep:, 2 * A_WIDTH:3 * A_WIDTH].reshape(bsz, keep, A_HEADS, A_HEAD_DIM)
        for j, o in enumerate((new_k, new_v, c1, n1, m1, buf)):
            outs[j].append(o)
        if li < len(c_layers):
            lc = c_layers[li]
            projc = _norm_inproj(x, lc["norm_pre"], lc["w_in"], zeros, zeros, n_rope=0, tm=TM_INPROJ)
            x, h_last, cbuf = _rglru_prompt(x, projc, lc, bsz, seq, TM_RGLRU)
            outs[6].append(h_last)
            outs[7].append(cbuf)
    return x.reshape(bsz, seq, D_MODEL), [jnp.stack(o, axis=0) for o in outs]


def _sample_trunk(x_sample, cache_k, cache_v, st, ab_layers, c_layers):
    bd = x_sample.shape[0]
    x = x_sample.reshape(bd, D_MODEL)
    cos_t, sin_t = _rope_tables(jnp.full((bd,), PAST_LEN, dtype=jnp.int32))
    zeros = jnp.zeros((bd, LANES), F32)
    outs = [[] for _ in range(8)]
    upds = []
    for li in range(len(ab_layers)):
        lw = ab_layers[li]
        proj = _norm_inproj(x, lw["norm_pre"], lw["w_in"], cos_t, sin_t, n_rope=2, tm=bd)
        qkv = proj[:, :3 * A_WIDTH].reshape(bd, 3 * A_HEADS, A_HEAD_DIM)
        oa = _decode_attn(qkv, cache_k, cache_v, li)
        dec, upd, n1, m1, buf = _mlstm_decode(proj, st[3][li], st[0], li, st[1][li], st[2][li], lw)
        upds.append(upd)
        new_k = qkv[:, A_HEADS:2 * A_HEADS].reshape(bd, 1, A_HEADS, A_HEAD_DIM)
        new_v = qkv[:, 2 * A_HEADS:].reshape(bd, 1, A_HEADS, A_HEAD_DIM)
        x = _ab_out_decode(x, proj, lw, oa.reshape(bd, A_WIDTH), dec)
        for j, o in zip((0, 1, 3, 4, 5), (new_k, new_v, n1, m1, buf)):
            outs[j].append(o)
        if li < len(c_layers):
            lc = c_layers[li]
            projc = _norm_inproj(x, lc["norm_pre"], lc["w_in"], zeros, zeros, n_rope=0, tm=bd)
            x, h, cbuf = _rglru_decode(x, projc, st[5][li], st[4][li], lc)
            outs[6].append(h)
            outs[7].append(cbuf)
    stacked = [None if j == 2 else jnp.stack(o, axis=0) for j, o in enumerate(outs)]
    stacked[2] = _mlstm_decode_cnew(st[0], upds)
    return x.reshape(bd, 1, D_MODEL), stacked


def kernel(x_prompt, x_sample, cache_k, cache_v, state_mlstm_c, state_mlstm_n, state_mlstm_m, state_mlstm_conv,
           state_rglru_h, state_rglru_conv, ab_norm_pre, ab_norm_post, ab_w_in, ab_w_out, mlstm_conv_w,
           mlstm_conv_b, mlstm_wq, mlstm_wk, mlstm_wv, mlstm_w_gates, mlstm_b_gates, mlstm_out_norm, mlstm_skip,
           c_norm_pre, c_norm_post, c_w_in, c_w_out, c_conv_w, c_conv_b, c_w_a, c_b_a, c_w_x, c_b_x, c_lambda):
    n_ab = ab_w_in.shape[0]
    n_c = c_w_in.shape[0]
    ab_layers = [_prep_ab(li, ab_norm_pre, ab_norm_post, ab_w_in, ab_w_out, mlstm_conv_w, mlstm_conv_b, mlstm_wq,
                          mlstm_wk, mlstm_wv, mlstm_w_gates, mlstm_b_gates, mlstm_out_norm, mlstm_skip)
                 for li in range(n_ab)]
    c_layers = [_prep_c(li, c_norm_pre, c_norm_post, c_w_in, c_w_out, c_conv_w, c_conv_b, c_w_a, c_b_a, c_w_x,
                        c_b_x, c_lambda) for li in range(n_c)]
    y_prompt, pst = _prompt_trunk(x_prompt, ab_layers, c_layers)
    st = (state_mlstm_c, state_mlstm_n, state_mlstm_m, state_mlstm_conv, state_rglru_h, state_rglru_conv)
    y_sample, sst = _sample_trunk(x_sample, cache_k, cache_v, st, ab_layers, c_layers)
    return (y_prompt, y_sample, *pst, *sst)
```

```python
import functools

import jax
import jax.numpy as jnp
from jax import lax
from jax.experimental import pallas as pl
from jax.experimental.pallas import tpu as pltpu

F32 = jnp.float32
BF16 = jnp.bfloat16

D_MODEL = 1024
PAST_LEN = 2048
A_HEADS = 8
A_HEAD_DIM = 128
A_WIDTH = 1024
A_PATTERNS = ((128, 1), (512, 4), (2048, 16))
A_MAX_WINDOW = 2048
ROPE_THETA = 10000.0
B_HEADS = 4
B_WIDTH = 1024
B_HEAD_DIM = 256
B_QKV_BLOCK = 4
B_CHUNK = 128
C_WIDTH = 1536
C_BLOCKS = 12
C_BLOCK = 128
RG_LRU_C = 8.0
CONV_WIDTH = 4
AB_IN = 6 * 1024
C_IN = 2 * C_WIDTH
NORM_EPS = 1e-6

LANES = 128
SUBLANES = 8
VMEM_LIMIT = 56 * 1024 * 1024

TM_INPROJ = 512
TM_RGLRU = 512
ATT_TQ = A_MAX_WINDOW
ATT_BLK = 128
ATT_QB = 128
NB_DECODE_ATTN = 4
NB_DECODE_STATE = 32


def _cparams(*sem):
    return pltpu.CompilerParams(dimension_semantics=sem, vmem_limit_bytes=VMEM_LIMIT)


def _sigmoid(x):
    return 0.5 * jnp.tanh(0.5 * x) + 0.5


def _causal_conv(x, xs_scr, cw_ref, cb_ref):
    t, w = x.shape
    halo = SUBLANES
    pitch = 2 * (t + halo)
    parts = []
    for g in range(w // LANES):
        ls = slice(g * LANES, (g + 1) * LANES)
        base = g * pitch
        xg = x[:, ls]
        xs_scr[pl.ds(base + 2 * halo, t, stride=2), :] = xg
        acc = xg * cw_ref[CONV_WIDTH - 1:CONV_WIDTH, ls] + cb_ref[:, ls]
        for s in range(1, CONV_WIDTH):
            acc = acc + xs_scr[pl.ds(base + 2 * (halo - s), t, stride=2), :] * cw_ref[CONV_WIDTH - 1 - s:CONV_WIDTH - s, ls]
        xs_scr[pl.ds(base, halo, stride=2), :] = xg[t - halo:t, :]
        parts.append(acc)
    return jnp.concatenate(parts, axis=1)


def _silu(x):
    return x * _sigmoid(x)


def _log_sigmoid(x):
    return jnp.minimum(x, 0.0) - jnp.log1p(jnp.exp(-jnp.abs(x)))


def _softplus(x):
    return jnp.maximum(x, 0.0) + jnp.log1p(jnp.exp(-jnp.abs(x)))


def _rmsnorm(x, g):
    return x * lax.rsqrt(jnp.mean(x * x, axis=-1, keepdims=True) + NORM_EPS) * g


def _dot(a, b):
    return jnp.dot(a, b, preferred_element_type=F32)


def _dot_nt(a, b):
    return lax.dot_general(a, b, (((1,), (1,)), ((), ())), preferred_element_type=F32)


def _dot_tn(a, b):
    return lax.dot_general(a, b, (((0,), (0,)), ((), ())), preferred_element_type=F32)


def _inproj_kernel(x_ref, g_ref, w_ref, cos_ref, sin_ref, o_ref, *, n_rope):
    h = _rmsnorm(x_ref[...], g_ref[...]).astype(BF16)
    group = A_WIDTH
    for j in range(w_ref.shape[1] // group):
        p = _dot(h, w_ref[:, j * group:(j + 1) * group])
        if j < n_rope:
            c = cos_ref[...]
            s = sin_ref[...]
            for hd in range(group // A_HEAD_DIM):
                ph = p[:, hd * A_HEAD_DIM:(hd + 1) * A_HEAD_DIM]
                lo = j * group + hd * A_HEAD_DIM
                o_ref[:, lo:lo + A_HEAD_DIM] = ph * c + pltpu.roll(ph, A_HEAD_DIM // 2, 1) * s
        else:
            o_ref[:, j * group:(j + 1) * group] = p


def _norm_inproj(x, g, w_bf16, cos_t, sin_t, *, n_rope, tm):
    t, d = x.shape
    n = w_bf16.shape[1]
    n_pos_tiles = cos_t.shape[0] // tm
    return pl.pallas_call(
        functools.partial(_inproj_kernel, n_rope=n_rope),
        grid=(t // tm,),
        in_specs=[
            pl.BlockSpec((tm, d), lambda i: (i, 0)),
            pl.BlockSpec((1, d), lambda i: (0, 0)),
            pl.BlockSpec((d, n), lambda i: (0, 0), pipeline_mode=pl.Buffered(1)),
            pl.BlockSpec((tm, LANES), lambda i: (i % n_pos_tiles, 0)),
            pl.BlockSpec((tm, LANES), lambda i: (i % n_pos_tiles, 0)),
        ],
        out_specs=pl.BlockSpec((tm, n), lambda i: (i, 0)),
        out_shape=jax.ShapeDtypeStruct((t, n), F32),
        compiler_params=_cparams("parallel"),
        name="norm_inproj",
    )(x, g.reshape(1, d), w_bf16, cos_t, sin_t)


def _attend(q, k, v, bias):
    sc = _dot_nt(q, k) * (A_HEAD_DIM ** -0.5) + bias
    m = jnp.max(sc, axis=-1, keepdims=True)
    p = jnp.exp(sc - m)
    l = jnp.sum(p, axis=-1, keepdims=True)
    o = _dot(p.astype(BF16), v) / l
    return o, jnp.broadcast_to(m + jnp.log(l), (q.shape[0], LANES))


def _attn_kernel(q_ref, k_ref, v_ref, o_ref, d4f, q1, q4, q16, k1e, k4e, k16e, v1e, v4e, v16e,
                 o4n, o16n, l4n, l16n):
    i = pl.program_id(2)
    blk = ATT_BLK
    tq = ATT_TQ
    n4 = tq // 4
    n16 = tq // 16

    @pl.when(i == 0)
    def _():
        for r in (k1e, v1e):
            r[0:blk, :] = jnp.zeros((blk, LANES), BF16)
        for r in (k4e, v4e, k16e, v16e):
            r[:, 0:blk, :] = jnp.zeros((r.shape[0], blk, LANES), BF16)

    @pl.when(i > 0)
    def _():
        for r in (k1e, v1e):
            r[0:blk, :] = r[tq:tq + blk, :]
        for r in (k4e, v4e):
            r[:, 0:blk, :] = r[:, n4:n4 + blk, :]
        for r in (k16e, v16e):
            r[:, 0:blk, :] = r[:, n16:n16 + blk, :]

    def split(src_ref, d1, d4, d16, off):
        def natural(c, carry):
            r0 = pl.multiple_of(c * 2 * blk, 2 * blk)
            d1[pl.ds(off + r0, 2 * blk), :] = src_ref[0, pl.ds(r0, 2 * blk), :].astype(BF16)
            return carry
        lax.fori_loop(0, tq // (2 * blk), natural, 0)
        for r4 in range(4):
            for c in range(n4 // blk):
                x = src_ref[0, pl.ds(r4 + 4 * c * blk, blk, stride=4), :]
                d4f[r4 * n4 + c * blk:r4 * n4 + (c + 1) * blk, :] = x
                d4[r4, off + c * blk:off + (c + 1) * blk, :] = x.astype(BF16)
        for r4 in range(4):
            for a in range(4):
                x = d4f[pl.ds(r4 * n4 + a, n16, stride=4), :]
                d16[r4 + 4 * a, off:off + n16, :] = x.astype(BF16)

    split(q_ref, q1, q4, q16, 0)
    split(k_ref, k1e, k4e, k16e, blk)
    split(v_ref, v1e, v4e, v16e, blk)

    qb = ATT_QB
    row = lax.broadcasted_iota(jnp.int32, (qb, qb + blk), 0)
    col = lax.broadcasted_iota(jnp.int32, (qb, qb + blk), 1)
    bias = jnp.where((col >= row) & (col - row <= blk), 0.0, -jnp.inf)
    bias_at = [jnp.where(col >= jnp.where(i == 0, blk - s * qb, 0), bias, -jnp.inf) for s in range(blk // qb)]
    bias_at += [bias] * (tq // qb)

    for r in range(16):
        for s in range(n16 // qb):
            win = slice(s * qb, (s + 1) * qb + blk)
            o, lse = _attend(q16[r, s * qb:(s + 1) * qb, :], k16e[r, win, :], v16e[r, win, :], bias_at[s])
            o16n[pl.ds(16 * s * qb + r, qb, stride=16), :] = o
            l16n[pl.ds(16 * s * qb + r, qb, stride=16), :] = lse

    for r4 in range(4):
        for s in range(n4 // qb):
            win = slice(s * qb, (s + 1) * qb + blk)
            o, lse = _attend(q4[r4, s * qb:(s + 1) * qb, :], k4e[r4, win, :], v4e[r4, win, :], bias_at[s])
            o4n[pl.ds(4 * s * qb + r4, qb, stride=4), :] = o
            l4n[pl.ds(4 * s * qb + r4, qb, stride=4), :] = lse

    for s in range(tq // qb):
        rows = slice(s * qb, (s + 1) * qb)
        win = slice(s * qb, (s + 1) * qb + blk)
        o1, l1 = _attend(q1[rows, :], k1e[win, :], v1e[win, :], bias_at[s])
        l4, l16 = l4n[rows, :], l16n[rows, :]
        mx = jnp.maximum(jnp.maximum(l1, l4), l16)
        e1, e4, e16 = jnp.exp(l1 - mx), jnp.exp(l4 - mx), jnp.exp(l16 - mx)
        o_ref[0, rows, :] = (o1 * e1 + o4n[rows, :] * e4 + o16n[rows, :] * e16) / (e1 + e4 + e16)


def _dilated_attn(proj, bsz, seq):
    tq, blk = ATT_TQ, ATT_BLK
    view = proj.reshape(bsz, seq, AB_IN)
    col = lambda c: pl.BlockSpec((1, tq, A_HEAD_DIM), lambda b, h, i: (b, i, c * A_HEADS + h))
    bf = lambda *s: pltpu.VMEM(s, BF16)
    nat = pltpu.VMEM((tq, LANES), F32)
    out = pl.pallas_call(
        _attn_kernel,
        grid=(bsz, A_HEADS, seq // tq),
        in_specs=[col(0), col(1), col(2)],
        out_specs=pl.BlockSpec((1, tq, A_HEAD_DIM), lambda b, h, i: (b, i, h)),
        out_shape=jax.ShapeDtypeStruct((bsz, seq, A_WIDTH), F32),
        scratch_shapes=[nat,
                        bf(tq, LANES), bf(4, tq // 4, LANES), bf(16, tq // 16, LANES),
                        bf(tq + blk, LANES), bf(4, tq // 4 + blk, LANES), bf(16, tq // 16 + blk, LANES),
                        bf(tq + blk, LANES), bf(4, tq // 4 + blk, LANES), bf(16, tq // 16 + blk, LANES),
                        nat, nat, nat, nat],
        compiler_params=_cparams("parallel", "parallel", "arbitrary"),
        name="dilated_attn",
    )(view, view, view)
    return out.reshape(bsz * seq, A_WIDTH)


def _decode_attn_kernel(q_ref, kn_ref, vn_ref, k1_ref, k4_ref, k16_ref, v1_ref, v4_ref, v16_ref, o_ref, *, nb):
    scale = A_HEAD_DIM ** -0.5
    for bb in range(nb):
        q = q_ref[bb]
        kn = kn_ref[bb]
        vn = vn_ref[bb]
        s_new = jnp.sum(q * kn, axis=-1, keepdims=True) * scale
        outs, lses = [], []
        for k_ref, v_ref in ((k1_ref, v1_ref), (k4_ref, v4_ref), (k16_ref, v16_ref)):
            kk = k_ref[0, bb, :, 0]
            vv = v_ref[0, bb, :, 0]
            sc = jnp.sum(kk * q[None], axis=-1, keepdims=True) * scale
            m = jnp.maximum(jnp.max(sc, axis=0), s_new)
            p = jnp.exp(sc - m[None])
            p_new = jnp.exp(s_new - m)
            l = jnp.sum(p, axis=0) + p_new
            outs.append((jnp.sum(p * vv, axis=0) + p_new * vn) / l)
            lses.append(m + jnp.log(l))
        mx = jnp.maximum(jnp.maximum(lses[0], lses[1]), lses[2])
        e = [jnp.exp(x - mx) for x in lses]
        tot = e[0] + e[1] + e[2]
        o_ref[bb] = (outs[0] * e[0] + outs[1] * e[1] + outs[2] * e[2]) / tot


def _decode_attn(qkv, cache_k, cache_v, li):
    bd = qkv.shape[0]
    n_layers, _, win = cache_k.shape[:3]
    nb = NB_DECODE_ATTN
    new = lambda c: pl.BlockSpec((nb, A_HEADS, A_HEAD_DIM), lambda i: (i, c, 0))
    specs, views = [], []
    for cache in (cache_k, cache_v):
        for window, dil in A_PATTERNS:
            rows = win // dil
            n_keys = window // dil
            views.append(cache.reshape(n_layers, bd, rows, dil, A_HEADS, A_HEAD_DIM))
            specs.append(pl.BlockSpec((1, nb, n_keys, 1, A_HEADS, A_HEAD_DIM),
                                      lambda i, rb=rows // n_keys - 1: (li, i, rb, 0, 0, 0)))
    return pl.pallas_call(
        functools.partial(_decode_attn_kernel, nb=nb),
        grid=(bd // nb,),
        in_specs=[new(0), new(1), new(2)] + specs,
        out_specs=pl.BlockSpec((nb, A_HEADS, A_HEAD_DIM), lambda i: (i, 0, 0)),
        out_shape=jax.ShapeDtypeStruct((bd, A_HEADS, A_HEAD_DIM), F32),
        compiler_params=_cparams("parallel"),
        name="decode_attn",
    )(qkv, qkv, qkv, *views)


def _blockdiag_dense(w):
    nb = w.shape[0]
    per = LANES // B_QKV_BLOCK
    w4 = w.reshape(nb // per, per, B_QKV_BLOCK, B_QKV_BLOCK)
    return jnp.einsum('gnio,nm->gnimo', w4, jnp.eye(per, dtype=w.dtype)).reshape(nb // per, LANES, LANES)


def _hi_lo(x):
    hi = x.astype(BF16)
    return hi, (x - hi.astype(F32)).astype(BF16)


def _blockdiag_mxu(x, whi_ref, wlo_ref):
    outs = []
    for g in range(x.shape[1] // LANES):
        hi, lo = _hi_lo(x[:, g * LANES:(g + 1) * LANES])
        outs.append(_dot(hi, whi_ref[g]) + _dot(lo, whi_ref[g]) + _dot(hi, wlo_ref[g]))
    return outs


def _mlstm_qkv(xb, xc, wqk_hi, wqk_lo, wv_hi, wv_lo):
    qk = _blockdiag_mxu(xc, wqk_hi, wqk_lo)
    qb = jnp.concatenate([o[:, :LANES] for o in qk], axis=1)
    kb = jnp.concatenate([o[:, LANES:] for o in qk], axis=1)
    vb = jnp.concatenate(_blockdiag_mxu(xb, wv_hi, wv_lo), axis=1)
    return qb, kb, vb


def _head_out(hb_raw, xc, on_ref, sk_ref):
    parts = []
    for h in range(B_HEADS):
        xh = hb_raw[:, h * B_HEAD_DIM:(h + 1) * B_HEAD_DIM]
        xcn = xh - jnp.mean(xh, axis=-1, keepdims=True)
        parts.append(xcn * lax.rsqrt(jnp.mean(xcn * xcn, axis=-1, keepdims=True) + 1e-5))
    return jnp.concatenate(parts, axis=1) * on_ref[...] + sk_ref[...] * xc


def _cumsum_lanes(x):
    t = x.shape[1]
    idx = lax.broadcasted_iota(jnp.int32, x.shape, 1)
    k = 1
    while k < t:
        x = x + jnp.where(idx >= k, pltpu.roll(x, k, 1), 0.0)
        k *= 2
    return x


def _mlstm_chunk(xc, qb, kb, qh, vh, gt, hraw_scr, c_scr, n_scr, m_scr, on_ref, sk_ref):
    t = B_CHUNK
    ng = 2 * B_HEADS
    brow = _cumsum_lanes(_log_sigmoid(gt))
    gcol = jnp.concatenate([gt, brow, jnp.zeros((t - 2 * ng, t), F32)], axis=0).T
    ri = lax.broadcasted_iota(jnp.int32, (t, t), 0)
    cj = lax.broadcasted_iota(jnp.int32, (t, t), 1)
    causal = cj <= ri
    kscale = B_HEAD_DIM ** -0.5
    for h in range(B_HEADS):
        sl = slice(h * B_HEAD_DIM, (h + 1) * B_HEAD_DIM)
        q = qb[:, sl]
        k = kb[:, sl] * kscale
        qhh = qh[:, sl]
        khh = k.astype(BF16)
        vhh = vh[:, sl]
        logi_c = gcol[:, h:h + 1]
        b_c = gcol[:, ng + B_HEADS + h:ng + B_HEADS + h + 1]
        logi_r = gt[h:h + 1, :]
        b_r = brow[B_HEADS + h:B_HEADS + h + 1, :]
        m0 = m_scr[h:h + 1, 0:1]
        n0 = n_scr[h:h + 1, :]
        c0 = c_scr[h]
        dmat = jnp.where(causal, b_c - b_r + logi_r, -jnp.inf)
        inter = b_c + m0
        m = jnp.maximum(jnp.max(dmat, axis=1, keepdims=True), inter)
        w_intra = jnp.exp(dmat - m)
        w_inter = jnp.exp(inter - m)
        s = _dot_nt(qhh, khh) * w_intra
        num = _dot(s.astype(BF16), vhh) + w_inter * _dot(qhh, c0.astype(BF16))
        den = jnp.sum(s, axis=1, keepdims=True) + w_inter * jnp.sum(q * n0, axis=1, keepdims=True)
        hraw_scr[:, sl] = num / jnp.maximum(jnp.abs(den), jnp.exp(-m))
        bl = b_c[t - 1:t, :]
        dl = bl - b_c + logi_c
        m_new = jnp.maximum(bl + m0, jnp.max(dl, axis=0, keepdims=True))
        kw = jnp.exp(dl - m_new) * k
        carry = jnp.exp(bl + m0 - m_new)
        c_scr[h] = carry * c0 + _dot_tn(kw.astype(BF16), vhh)
        n_scr[h:h + 1, :] = carry * n0 + jnp.sum(kw, axis=0, keepdims=True)
        m_scr[h:h + 1, :] = jnp.broadcast_to(m_new, (1, LANES))
    return _head_out(hraw_scr[...], xc, on_ref, sk_ref)


def _gate_project(oa, hb, za, zb, x, w_ref, g_ref):
    ya = oa * _silu(za)
    yb = hb * _silu(zb)
    y = _dot(ya.astype(BF16), w_ref[0:A_WIDTH, :]) + _dot(yb.astype(BF16), w_ref[A_WIDTH:A_WIDTH + B_WIDTH, :])
    return x + _rmsnorm(y, g_ref[...])


def _mlstm_prompt_kernel(xb_ref, oa_ref, za_ref, zb_ref, x_ref, cw_ref, cb_ref, wqk_hi, wqk_lo, wv_hi, wv_lo,
                         wgt_ref, bgt_ref, on_ref, sk_ref, w_ref, g_ref,
                         o_ref, c_out, n_out, m_out, buf_out,
                         xx_scr, hraw_scr, c_scr, n_scr, m_scr):
    ci = pl.program_id(0)
    nc = pl.num_programs(0)
    t = B_CHUNK
    halo = SUBLANES
    bsz = xb_ref.shape[0]

    @pl.when(ci == 0)
    def _():
        xx_scr[...] = jnp.zeros_like(xx_scr)
        c_scr[...] = jnp.zeros_like(c_scr)
        n_scr[...] = jnp.zeros_like(n_scr)
        m_scr[...] = jnp.zeros_like(m_scr)

    rows = lambda r: r[...].reshape(bsz * t, r.shape[-1])
    x = rows(xb_ref)
    xc = jnp.concatenate([_silu(_causal_conv(xb_ref[b], xx_scr.at[b], cw_ref, cb_ref)) for b in range(bsz)],
                         axis=0)
    qb, kb, vb = _mlstm_qkv(x, xc, wqk_hi, wqk_lo, wv_hi, wv_lo)
    qh, kh, vh = qb.astype(BF16), kb.astype(BF16), vb.astype(BF16)
    gt = (_dot_nt(wgt_ref[:, 0:B_WIDTH], qh) + _dot_nt(wgt_ref[:, B_WIDTH:2 * B_WIDTH], kh)
          + _dot_nt(wgt_ref[:, 2 * B_WIDTH:3 * B_WIDTH], vh) + bgt_ref[...])
    hb = []
    for b in range(bsz):
        rs = slice(b * t, (b + 1) * t)
        hb.append(_mlstm_chunk(xc[rs], qb[rs], kb[rs], qh[rs], vh[rs], gt[:, rs], hraw_scr.at[b], c_scr.at[b],
                               n_scr.at[b], m_scr.at[b], on_ref, sk_ref))
    y = _gate_project(rows(oa_ref), jnp.concatenate(hb, axis=0), rows(za_ref), rows(zb_ref), rows(x_ref),
                      w_ref, g_ref)
    o_ref[...] = y.reshape(bsz, t, D_MODEL)

    @pl.when(ci == nc - 1)
    def _():
        c_out[...] = c_scr[...]
        n_out[...] = n_scr[:, 0:B_HEADS, :]
        lane = lax.broadcasted_iota(jnp.int32, (1, LANES), 1)
        for b in range(bsz):
            mt = jnp.zeros((1, LANES), F32)
            for h in range(B_HEADS):
                mt = jnp.where(lane == h, m_scr[b, h:h + 1, :], mt)
            m_out[b] = mt
            buf_out[b] = xb_ref[b, t - halo:t, :]


def _mlstm_prompt(x, proj, oa, lw, bsz, seq):
    t = B_CHUNK
    view = proj.reshape(bsz, seq, AB_IN)
    full = lambda a: pl.BlockSpec(a.shape, lambda c: (0,) * a.ndim)
    pcol = lambda col: pl.BlockSpec((bsz, t, B_WIDTH), lambda c: (0, c, col))
    row = pl.BlockSpec((bsz, t, D_MODEL), lambda c: (0, c, 0))
    whole = lambda *s: pl.BlockSpec(s, lambda c: (0,) * len(s))
    params = [lw["conv_w"], lw["conv_b"], lw["wqk_hi"], lw["wqk_lo"], lw["wv_hi"], lw["wv_lo"], lw["wgt"],
              lw["bgt"], lw["out_norm"], lw["skip"], lw["w_out"], lw["norm_post"]]
    sds = jax.ShapeDtypeStruct
    y, c, n, m, buf = pl.pallas_call(
        _mlstm_prompt_kernel,
        grid=(seq // t,),
        in_specs=[pcol(4), row, pcol(3), pcol(5), row] + [full(a) for a in params],
        out_specs=[row,
                   whole(bsz, B_HEADS, B_HEAD_DIM, B_HEAD_DIM),
                   whole(bsz, B_HEADS, B_HEAD_DIM),
                   whole(bsz, 1, LANES),
                   whole(bsz, SUBLANES, B_WIDTH)],
        out_shape=[sds((bsz, seq, D_MODEL), F32),
                   sds((bsz, B_HEADS, B_HEAD_DIM, B_HEAD_DIM), F32),
                   sds((bsz, B_HEADS, B_HEAD_DIM), F32),
                   sds((bsz, 1, LANES), F32),
                   sds((bsz, SUBLANES, B_WIDTH), F32)],
        scratch_shapes=[pltpu.VMEM((bsz, (B_WIDTH // LANES) * 2 * (t + SUBLANES), LANES), F32),
                        pltpu.VMEM((bsz, t, B_WIDTH), F32),
                        pltpu.VMEM((bsz, B_HEADS, B_HEAD_DIM, B_HEAD_DIM), F32),
                        pltpu.VMEM((bsz, SUBLANES, B_HEAD_DIM), F32),
                        pltpu.VMEM((bsz, SUBLANES, LANES), F32)],
        compiler_params=_cparams("arbitrary"),
        name="mlstm_prompt",
    )(view, oa.reshape(bsz, seq, A_WIDTH), view, view, x.reshape(bsz, seq, D_MODEL), *params)
    return (y.reshape(bsz * seq, D_MODEL), c, n, m[:, 0, :B_HEADS], buf[:, SUBLANES - (CONV_WIDTH - 1):, :])


def _mlstm_decode_pre_kernel(xb_ref, b0_ref, b1_ref, b2_ref, cw_ref, cb_ref, wqk_hi, wqk_lo, wv_hi, wv_lo, wg_ref,
                             bg_ref, n0_ref, m0_ref,
                             xc_ref, v_ref, sv_ref, wi_ref, den_ref, n_out, m_out, buf_out, qt_ref, kwt_ref):
    x = xb_ref[...]
    xconv = (x * cw_ref[3:4, :] + b2_ref[...] * cw_ref[2:3, :] + b1_ref[...] * cw_ref[1:2, :]
             + b0_ref[...] * cw_ref[0:1, :] + cb_ref[...])
    xc = _silu(xconv)
    xc_ref[...] = xc
    buf_out[:, 0:B_WIDTH] = b1_ref[...]
    buf_out[:, B_WIDTH:2 * B_WIDTH] = b2_ref[...]
    buf_out[:, 2 * B_WIDTH:3 * B_WIDTH] = x
    qb, kb, vb = _mlstm_qkv(x, xc, wqk_hi, wqk_lo, wv_hi, wv_lo)
    g = (_dot(qb.astype(BF16), wg_ref[0:B_WIDTH, :]) + _dot(kb.astype(BF16), wg_ref[B_WIDTH:2 * B_WIDTH, :])
         + _dot(vb.astype(BF16), wg_ref[2 * B_WIDTH:3 * B_WIDTH, :]) + bg_ref[...])
    v_ref[...] = vb
    logf = _log_sigmoid(g)
    rows = x.shape[0]
    lane = lax.broadcasted_iota(jnp.int32, (rows, LANES), 1)
    m_tile = jnp.zeros((rows, LANES), F32)
    kscale = B_HEAD_DIM ** -0.5
    for h in range(B_HEADS):
        sl = slice(h * B_HEAD_DIM, (h + 1) * B_HEAD_DIM)
        logi = g[:, h:h + 1]
        fm = logf[:, B_HEADS + h:B_HEADS + h + 1] + m0_ref[:, h:h + 1]
        m = jnp.maximum(logi, fm)
        w_intra = jnp.exp(logi - m)
        w_inter = jnp.exp(fm - m)
        q = qb[:, sl]
        k = kb[:, sl] * kscale
        n0 = n0_ref[:, sl]
        s = jnp.sum(q * k, axis=1, keepdims=True) * w_intra
        den = s + w_inter * jnp.sum(q * n0, axis=1, keepdims=True)
        kw = w_intra * k
        sv_ref[:, sl] = s * vb[:, sl]
        wi_ref[:, sl] = jnp.broadcast_to(w_inter, q.shape)
        den_ref[:, sl] = jnp.broadcast_to(jnp.maximum(jnp.abs(den), jnp.exp(-m)), q.shape)
        n_out[:, sl] = w_inter * n0 + kw
        m_tile = jnp.where(lane == h, m, m_tile)
        qt_ref[h] = q.T
        kwt_ref[h] = kw.T
    m_out[...] = m_tile


def _column(mat_t, b):
    lane = lax.broadcasted_iota(jnp.int32, mat_t.shape, 1)
    return jnp.sum(jnp.where(lane == b, mat_t, 0.0), axis=1, keepdims=True)


def _mlstm_decode_qc_kernel(c0_ref, qt_ref, qc_ref, *, nb):
    i = pl.program_id(1)
    qt = qt_ref[0]
    for bb in range(nb):
        qc_ref[bb] = jnp.sum(c0_ref[0, bb, 0] * _column(qt, i * nb + bb), axis=0, keepdims=True)


def _mlstm_decode_cnew_kernel(c0_ref, kwt_ref, v_ref, wi_ref, c_out, *, nb):
    i = pl.program_id(2)
    kwt = kwt_ref[0, 0]
    for bb in range(nb):
        c_out[0, bb, 0] = wi_ref[0, bb] * c0_ref[0, bb, 0] + _column(kwt, i * nb + bb) * v_ref[0, bb]


def _mlstm_decode(proj, conv_state, state_c, li, n0, m0, lw):
    bd = proj.shape[0]
    w = B_WIDTH
    cs = conv_state.reshape(bd, (CONV_WIDTH - 1) * w)
    m0p = jnp.pad(m0, ((0, 0), (0, LANES - B_HEADS)))
    row = lambda c: pl.BlockSpec((bd, w), lambda i: (0, c))
    full = lambda a: pl.BlockSpec(a.shape, lambda i: (0,) * a.ndim)
    params = [lw["conv_w"], lw["conv_b"], lw["wqk_hi"], lw["wqk_lo"], lw["wv_hi"], lw["wv_lo"], lw["wg"], lw["bg"]]
    n0f = n0.reshape(bd, w)
    sds = jax.ShapeDtypeStruct
    tr = pl.BlockSpec((B_HEADS, B_HEAD_DIM, bd), lambda i: (0, 0, 0))
    outs = pl.pallas_call(
        _mlstm_decode_pre_kernel,
        grid=(1,),
        in_specs=[row(4), row(0), row(1), row(2)] + [full(a) for a in params] + [full(n0f), full(m0p)],
        out_specs=[row(0)] * 6 + [pl.BlockSpec((bd, LANES), lambda i: (0, 0)),
                                  pl.BlockSpec((bd, 3 * w), lambda i: (0, 0)), tr, tr],
        out_shape=[sds((bd, w), F32)] * 6 + [sds((bd, LANES), F32), sds((bd, 3 * w), F32),
                                             sds((B_HEADS, B_HEAD_DIM, bd), F32),
                                             sds((B_HEADS, B_HEAD_DIM, bd), F32)],
        compiler_params=_cparams("arbitrary"),
        name="mlstm_decode_pre",
    )(proj, cs, cs, cs, *params, n0f, m0p)
    xc, v, sv, wi, den, n_new, m_new, buf, qt, kwt = outs
    nb = NB_DECODE_STATE
    qc = pl.pallas_call(
        functools.partial(_mlstm_decode_qc_kernel, nb=nb),
        grid=(B_HEADS, bd // nb),
        in_specs=[pl.BlockSpec((1, nb, 1, B_HEAD_DIM, B_HEAD_DIM), lambda h, i: (li, i, h, 0, 0)),
                  pl.BlockSpec((1, B_HEAD_DIM, bd), lambda h, i: (h, 0, 0))],
        out_specs=pl.BlockSpec((nb, 1, B_HEAD_DIM), lambda h, i: (i, 0, h)),
        out_shape=sds((bd, 1, w), F32),
        compiler_params=_cparams("parallel", "arbitrary"),
        name="mlstm_decode_qc",
    )(state_c, qt)
    dec = dict(xc=xc, sv=sv, wi=wi, den=den, qc=qc.reshape(bd, w))
    upd = dict(kwt=kwt, v=v.reshape(bd, 1, w), wi=wi.reshape(bd, 1, w))
    return (dec, upd, n_new.reshape(bd, B_HEADS, B_HEAD_DIM), m_new[:, :B_HEADS],
            buf.reshape(bd, CONV_WIDTH - 1, w))


def _mlstm_decode_cnew(state_c, upds):
    n_layers, bd = state_c.shape[:2]
    nb = NB_DECODE_STATE
    kwt = jnp.stack([u["kwt"] for u in upds], axis=0)
    v = jnp.stack([u["v"] for u in upds], axis=0)
    wi = jnp.stack([u["wi"] for u in upds], axis=0)
    cblk = pl.BlockSpec((1, nb, 1, B_HEAD_DIM, B_HEAD_DIM), lambda l, h, i: (l, i, h, 0, 0))
    rowh = pl.BlockSpec((1, nb, 1, B_HEAD_DIM), lambda l, h, i: (l, i, 0, h))
    return pl.pallas_call(
        functools.partial(_mlstm_decode_cnew_kernel, nb=nb),
        grid=(n_layers, B_HEADS, bd // nb),
        in_specs=[cblk, pl.BlockSpec((1, 1, B_HEAD_DIM, bd), lambda l, h, i: (l, h, 0, 0)), rowh, rowh],
        out_specs=cblk,
        out_shape=jax.ShapeDtypeStruct(state_c.shape, F32),
        compiler_params=_cparams("parallel", "parallel", "arbitrary"),
        name="mlstm_decode_cnew",
    )(state_c, kwt, v, wi)


def _ab_out_decode_kernel(oa_ref, sv_ref, wi_ref, qc_ref, den_ref, xc_ref, za_ref, zb_ref, x_ref,
                          w_ref, on_ref, sk_ref, g_ref, o_ref):
    hb = _head_out((sv_ref[...] + wi_ref[...] * qc_ref[...]) / den_ref[...], xc_ref[...], on_ref, sk_ref)
    o_ref[...] = _gate_project(oa_ref[...], hb, za_ref[...], zb_ref[...], x_ref[...], w_ref, g_ref)


def _ab_out_decode(x, proj, lw, oa, dec):
    bd, w = x.shape
    row = pl.BlockSpec((bd, w), lambda i: (0, 0))
    pcol = lambda c: pl.BlockSpec((bd, w), lambda i: (0, c))
    full = lambda a: pl.BlockSpec(a.shape, lambda i: (0,) * a.ndim)
    params = [lw["w_out"], lw["out_norm"], lw["skip"], lw["norm_post"]]
    return pl.pallas_call(
        _ab_out_decode_kernel,
        grid=(1,),
        in_specs=[row] * 6 + [pcol(3), pcol(5), row] + [full(a) for a in params],
        out_specs=row,
        out_shape=jax.ShapeDtypeStruct((bd, w), F32),
        compiler_params=_cparams("arbitrary"),
        name="ab_out_decode",
    )(oa, dec["sv"], dec["wi"], dec["qc"], dec["den"], dec["xc"], proj, proj, x, *params)


def _rglru_gates(xconv, wa_ref, ba_ref, wx_ref, bx_ref, lam_ref):
    rs, is_ = [], []
    for n in range(C_BLOCKS):
        sl = slice(n * C_BLOCK, (n + 1) * C_BLOCK)
        xa = xconv[:, sl].astype(BF16)
        rs.append(_dot(xa, wa_ref[n]))
        is_.append(_dot(xa, wx_ref[n]))
    r = _sigmoid(jnp.concatenate(rs, axis=1) + ba_ref[...])
    ig = _sigmoid(jnp.concatenate(is_, axis=1) + bx_ref[...])
    a = jnp.exp(r * (-RG_LRU_C * _softplus(-lam_ref[...])))
    u = jnp.sqrt(1.0 - a * a) * (ig * xconv)
    return a, u


def _rglru_out(h, z, x, w_ref, g_ref):
    y = _dot((h * _silu(z)).astype(BF16), w_ref[...])
    return x + _rmsnorm(y, g_ref[...])


def _rglru_prompt_kernel(xc_ref, zc_ref, x_ref, cw_ref, cb_ref, wa_ref, ba_ref, wx_ref, bx_ref, lam_ref,
                         w_ref, g_ref, o_ref, h_out, buf_out, xx_scr, a_scr, u_scr, hc_scr, *, tm):
    ti = pl.program_id(1)
    nt = pl.num_programs(1)
    halo = SUBLANES

    @pl.when(ti == 0)
    def _():
        xx_scr[...] = jnp.zeros_like(xx_scr)
        hc_scr[...] = jnp.zeros_like(hc_scr)

    x = xc_ref[0]
    xconv = _causal_conv(x, xx_scr, cw_ref, cb_ref)
    a, u = _rglru_gates(xconv, wa_ref, ba_ref, wx_ref, bx_ref, lam_ref)
    a_scr[...] = a
    u_scr[...] = u
    ridx = lax.broadcasted_iota(jnp.int32, (SUBLANES, C_WIDTH), 0)

    def body(r, hprev):
        r8 = pl.multiple_of(r * SUBLANES, SUBLANES)
        aa = a_scr[pl.ds(r8, SUBLANES), :]
        uu = u_scr[pl.ds(r8, SUBLANES), :]
        for sft in (1, 2, 4):
            keep = ridx >= sft
            a_sh = jnp.where(keep, pltpu.roll(aa, sft, 0), 1.0)
            u_sh = jnp.where(keep, pltpu.roll(uu, sft, 0), 0.0)
            uu = aa * u_sh + uu
            aa = aa * a_sh
        hblk = aa * hprev + uu
        u_scr[pl.ds(r8, SUBLANES), :] = hblk
        return hblk[SUBLANES - 1:SUBLANES, :]

    hlast = lax.fori_loop(0, tm // SUBLANES, body, hc_scr[0:1, :])
    hc_scr[0:1, :] = hlast
    o_ref[0] = _rglru_out(u_scr[...], zc_ref[0], x_ref[0], w_ref, g_ref)

    @pl.when(ti == nt - 1)
    def _():
        h_out[0] = hlast
        buf_out[0] = x[tm - halo:tm, :]


def _rglru_decode_kernel(xc_ref, zc_ref, x_ref, b0_ref, b1_ref, b2_ref, h0_ref, cw_ref, cb_ref, wa_ref, ba_ref,
                         wx_ref, bx_ref, lam_ref, w_ref, g_ref, o_ref, h_out, buf_out):
    x = xc_ref[...]
    xconv = (x * cw_ref[3:4, :] + b2_ref[...] * cw_ref[2:3, :] + b1_ref[...] * cw_ref[1:2, :]
             + b0_ref[...] * cw_ref[0:1, :] + cb_ref[...])
    a, u = _rglru_gates(xconv, wa_ref, ba_ref, wx_ref, bx_ref, lam_ref)
    h = a * h0_ref[...] + u
    h_out[...] = h
    buf_out[:, 0:C_WIDTH] = b1_ref[...]
    buf_out[:, C_WIDTH:2 * C_WIDTH] = b2_ref[...]
    buf_out[:, 2 * C_WIDTH:3 * C_WIDTH] = x
    o_ref[...] = _rglru_out(h, zc_ref[...], x_ref[...], w_ref, g_ref)


def _c_params(lw):
    return [lw["conv_w"], lw["conv_b"], lw["w_a"], lw["b_a"], lw["w_x"], lw["b_x"], lw["lam"],
            lw["w_out"], lw["norm_post"]]


def _rglru_prompt(x, proj, lw, bsz, seq, tm):
    cw = C_WIDTH
    pv = proj.reshape(bsz, seq, C_IN)
    xv = x.reshape(bsz, seq, D_MODEL)
    params = _c_params(lw)
    full = lambda a: pl.BlockSpec(a.shape, lambda b, i: (0,) * a.ndim)
    sds = jax.ShapeDtypeStruct
    y, h_last, buf = pl.pallas_call(
        functools.partial(_rglru_prompt_kernel, tm=tm),
        grid=(bsz, seq // tm),
        in_specs=[pl.BlockSpec((1, tm, cw), lambda b, i: (b, i, 0)),
                  pl.BlockSpec((1, tm, cw), lambda b, i: (b, i, 1)),
                  pl.BlockSpec((1, tm, D_MODEL), lambda b, i: (b, i, 0))] + [full(a) for a in params],
        out_specs=[pl.BlockSpec((1, tm, D_MODEL), lambda b, i: (b, i, 0)),
                   pl.BlockSpec((1, 1, cw), lambda b, i: (b, 0, 0)),
                   pl.BlockSpec((1, SUBLANES, cw), lambda b, i: (b, 0, 0))],
        out_shape=[sds((bsz, seq, D_MODEL), F32), sds((bsz, 1, cw), F32), sds((bsz, SUBLANES, cw), F32)],
        scratch_shapes=[pltpu.VMEM(((cw // LANES) * 2 * (tm + SUBLANES), LANES), F32), pltpu.VMEM((tm, cw), F32),
                        pltpu.VMEM((tm, cw), F32), pltpu.VMEM((SUBLANES, cw), F32)],
        compiler_params=_cparams("parallel", "arbitrary"),
        name="rglru_prompt",
    )(pv, pv, xv, *params)
    return y.reshape(bsz * seq, D_MODEL), h_last[:, 0, :], buf[:, SUBLANES - (CONV_WIDTH - 1):, :]


def _rglru_decode(x, proj, conv_state, h0, lw):
    bd = x.shape[0]
    cw = C_WIDTH
    cs = conv_state.reshape(bd, (CONV_WIDTH - 1) * cw)
    params = _c_params(lw)
    col = lambda c: pl.BlockSpec((bd, cw), lambda i: (0, c))
    full = lambda a: pl.BlockSpec(a.shape, lambda i: (0,) * a.ndim)
    sds = jax.ShapeDtypeStruct
    y, h, buf = pl.pallas_call(
        _rglru_decode_kernel,
        grid=(1,),
        in_specs=[col(0), col(1), full(x), col(0), col(1), col(2), full(h0)] + [full(a) for a in params],
        out_specs=[pl.BlockSpec((bd, D_MODEL), lambda i: (0, 0)), pl.BlockSpec((bd, cw), lambda i: (0, 0)),
                   pl.BlockSpec((bd, 3 * cw), lambda i: (0, 0))],
        out_shape=[sds((bd, D_MODEL), F32), sds((bd, cw), F32), sds((bd, 3 * cw), F32)],
        compiler_params=_cparams("arbitrary"),
        name="rglru_decode",
    )(proj, proj, x, cs, cs, cs, h0, *params)
    return y, h, buf.reshape(bd, CONV_WIDTH - 1, cw)


def _rope_tables(pos):
    half = A_HEAD_DIM // 2
    lane = jnp.arange(A_HEAD_DIM)
    freqs = ROPE_THETA ** (-(lane % half).astype(F32) / half)
    ang = pos.astype(F32)[:, None] * freqs[None, :]
    sign = jnp.where(lane < half, -1.0, 1.0).astype(F32)
    return jnp.cos(ang), jnp.sin(ang) * sign[None, :]


def _prep_ab(li, ab_norm_pre, ab_norm_post, ab_w_in, ab_w_out, mlstm_conv_w, mlstm_conv_b, mlstm_wq, mlstm_wk,
             mlstm_wv, mlstm_w_gates, mlstm_b_gates, mlstm_out_norm, mlstm_skip):
    ng = 2 * B_HEADS
    wg = jnp.pad(mlstm_w_gates[li], ((0, 0), (0, LANES - ng))).astype(BF16)
    wqk = jnp.concatenate([_blockdiag_dense(mlstm_wq[li]), _blockdiag_dense(mlstm_wk[li])], axis=2)
    wqk_hi, wqk_lo = _hi_lo(wqk)
    wv_hi, wv_lo = _hi_lo(_blockdiag_dense(mlstm_wv[li]))
    return dict(
        norm_pre=ab_norm_pre[li], norm_post=ab_norm_post[li].reshape(1, D_MODEL),
        w_in=ab_w_in[li].astype(BF16), w_out=ab_w_out[li].astype(BF16),
        conv_w=mlstm_conv_w[li], conv_b=mlstm_conv_b[li].reshape(1, B_WIDTH),
        wqk_hi=wqk_hi, wqk_lo=wqk_lo, wv_hi=wv_hi, wv_lo=wv_lo,
        wg=wg, wgt=mlstm_w_gates[li].T.astype(BF16),
        bg=jnp.pad(mlstm_b_gates[li], (0, LANES - ng)).reshape(1, LANES),
        bgt=mlstm_b_gates[li].reshape(ng, 1),
        out_norm=mlstm_out_norm[li].reshape(1, B_WIDTH), skip=mlstm_skip[li].reshape(1, B_WIDTH))


def _prep_c(li, c_norm_pre, c_norm_post, c_w_in, c_w_out, c_conv_w, c_conv_b, c_w_a, c_b_a, c_w_x, c_b_x,
            c_lambda):
    return dict(
        norm_pre=c_norm_pre[li], norm_post=c_norm_post[li].reshape(1, D_MODEL),
        w_in=c_w_in[li].astype(BF16), w_out=c_w_out[li].astype(BF16),
        conv_w=c_conv_w[li], conv_b=c_conv_b[li].reshape(1, C_WIDTH),
        w_a=c_w_a[li].astype(BF16), b_a=c_b_a[li].reshape(1, C_WIDTH),
        w_x=c_w_x[li].astype(BF16), b_x=c_b_x[li].reshape(1, C_WIDTH),
        lam=c_lambda[li].reshape(1, C_WIDTH))


def _prompt_trunk(x_prompt, ab_layers, c_layers):
    bsz, seq, _ = x_prompt.shape
    x = x_prompt.reshape(bsz * seq, D_MODEL)
    cos_t, sin_t = _rope_tables(jnp.arange(seq, dtype=jnp.int32))
    zeros = jnp.zeros((seq, LANES), F32)
    outs = [[] for _ in range(8)]
    keep = min(A_MAX_WINDOW, seq)
    for li in range(len(ab_layers)):
        lw = ab_layers[li]
        proj = _norm_inproj(x, lw["norm_pre"], lw["w_in"], cos_t, sin_t, n_rope=2, tm=TM_INPROJ)
        oa = _dilated_attn(proj, bsz, seq)
        x, c1, n1, m1, buf = _mlstm_prompt(x, proj, oa, lw, bsz, seq)
        pv = proj.reshape(bsz, seq, AB_IN)
        new_k = pv[:, seq - keep:, A_WIDTH:2 * A_WIDTH].reshape(bsz, keep, A_HEADS, A_HEAD_DIM)
        new_v = pv[:, seq - keep:, 2 * A_WIDTH:3 * A_WIDTH].reshape(bsz, keep, A_HEADS, A_HEAD_DIM)
        for j, o in enumerate((new_k, new_v, c1, n1, m1, buf)):
            outs[j].append(o)
        if li < len(c_layers):
            lc = c_layers[li]
            projc = _norm_inproj(x, lc["norm_pre"], lc["w_in"], zeros, zeros, n_rope=0, tm=TM_INPROJ)
            x, h_last, cbuf = _rglru_prompt(x, projc, lc, bsz, seq, TM_RGLRU)
            outs[6].append(h_last)
            outs[7].append(cbuf)
    return x.reshape(bsz, seq, D_MODEL), [jnp.stack(o, axis=0) for o in outs]


def _sample_trunk(x_sample, cache_k, cache_v, st, ab_layers, c_layers):
    bd = x_sample.shape[0]
    x = x_sample.reshape(bd, D_MODEL)
    cos_t, sin_t = _rope_tables(jnp.full((bd,), PAST_LEN, dtype=jnp.int32))
    zeros = jnp.zeros((bd, LANES), F32)
    outs = [[] for _ in range(8)]
    upds = []
    for li in range(len(ab_layers)):
        lw = ab_layers[li]
        proj = _norm_inproj(x, lw["norm_pre"], lw["w_in"], cos_t, sin_t, n_rope=2, tm=bd)
        qkv = proj[:, :3 * A_WIDTH].reshape(bd, 3 * A_HEADS, A_HEAD_DIM)
        oa = _decode_attn(qkv, cache_k, cache_v, li)
        dec, upd, n1, m1, buf = _mlstm_decode(proj, st[3][li], st[0], li, st[1][li], st[2][li], lw)
        upds.append(upd)
        new_k = qkv[:, A_HEADS:2 * A_HEADS].reshape(bd, 1, A_HEADS, A_HEAD_DIM)
        new_v = qkv[:, 2 * A_HEADS:].reshape(bd, 1, A_HEADS, A_HEAD_DIM)
        x = _ab_out_decode(x, proj, lw, oa.reshape(bd, A_WIDTH), dec)
        for j, o in zip((0, 1, 3, 4, 5), (new_k, new_v, n1, m1, buf)):
            outs[j].append(o)
        if li < len(c_layers):
            lc = c_layers[li]
            projc = _norm_inproj(x, lc["norm_pre"], lc["w_in"], zeros, zeros, n_rope=0, tm=bd)
            x, h, cbuf = _rglru_decode(x, projc, st[5][li], st[4][li], lc)
            outs[6].append(h)
            outs[7].append(cbuf)
    stacked = [None if j == 2 else jnp.stack(o, axis=0) for j, o in enumerate(outs)]
    stacked[2] = _mlstm_decode_cnew(st[0], upds)
    return x.reshape(bd, 1, D_MODEL), stacked


def kernel(x_prompt, x_sample, cache_k, cache_v, state_mlstm_c, state_mlstm_n, state_mlstm_m, state_mlstm_conv,
           state_rglru_h, state_rglru_conv, ab_norm_pre, ab_norm_post, ab_w_in, ab_w_out, mlstm_conv_w,
           mlstm_conv_b, mlstm_wq, mlstm_wk, mlstm_wv, mlstm_w_gates, mlstm_b_gates, mlstm_out_norm, mlstm_skip,
           c_norm_pre, c_norm_post, c_w_in, c_w_out, c_conv_w, c_conv_b, c_w_a, c_b_a, c_w_x, c_b_x, c_lambda):
    n_ab = ab_w_in.shape[0]
    n_c = c_w_in.shape[0]
    ab_layers = [_prep_ab(li, ab_norm_pre, ab_norm_post, ab_w_in, ab_w_out, mlstm_conv_w, mlstm_conv_b, mlstm_wq,
                          mlstm_wk, mlstm_wv, mlstm_w_gates, mlstm_b_gates, mlstm_out_norm, mlstm_skip)
                 for li in range(n_ab)]
    c_layers = [_prep_c(li, c_norm_pre, c_norm_post, c_w_in, c_w_out, c_conv_w, c_conv_b, c_w_a, c_b_a, c_w_x,
                        c_b_x, c_lambda) for li in range(n_c)]
    y_prompt, pst = _prompt_trunk(x_prompt, ab_layers, c_layers)
    st = (state_mlstm_c, state_mlstm_n, state_mlstm_m, state_mlstm_conv, state_rglru_h, state_rglru_conv)
    y_sample, sst = _sample_trunk(x_sample, cache_k, cache_v, st, ab_layers, c_layers)
    return (y_prompt, y_sample, *pst, *sst)
```
